```python
import math
import jax, jax.numpy as jnp
from jax import lax
import numpy as np

D_MODEL = 1024
BATCH = 4
SEQ = 4096
DEPTH = 2

CHUNK = 64
N_MIXERS = 2
N_CONV_LAYERS = (DEPTH + 1) // 2
N_MLA_LAYERS = DEPTH // 2

CONV_WIDTH = 31

MLA_HEADS = 8
MLA_NOPE = 128
MLA_ROPE = 64
MLA_V = 128
MLA_Q_RANK = 384
MLA_KV_RANK = 256
ROPE_THETA = 10000.0
Q_BLOCK = 128

PEER_HEADS = 8
PEER_N_KEYS = 128
PEER_N_EXPERTS = PEER_N_KEYS * PEER_N_KEYS
PEER_D_QUERY = 256
PEER_TOPK = 16
PEER_TOKEN_BLOCK = 128

LN_EPS = 1e-5
RMS_EPS = 1e-6
DEEPNORM_ALPHA = (2 * DEPTH) ** 0.25
DEEPNORM_BETA = (8 * DEPTH) ** -0.25

kernel_name = 'hybrid_conv_mla_peer_deepnorm'


def layer_norm(x, g, b):
    xf = x.astype(jnp.float32)
    mu = jnp.mean(xf, axis=-1, keepdims=True)
    var = jnp.mean(jnp.square(xf - mu), axis=-1, keepdims=True)
    y = (xf - mu) * lax.rsqrt(var + LN_EPS)
    return (y * g.astype(jnp.float32) + b.astype(jnp.float32)).astype(x.dtype)


def rms_norm(x, g):
    xf = x.astype(jnp.float32)
    y = xf * lax.rsqrt(jnp.mean(jnp.square(xf), axis=-1, keepdims=True) + RMS_EPS)
    return (y * g.astype(jnp.float32)).astype(x.dtype)


def rope_tables(positions):
    inv = 1.0 / (ROPE_THETA ** (jnp.arange(0, MLA_ROPE, 2, dtype=jnp.float32) / MLA_ROPE))
    ang = positions.astype(jnp.float32)[..., None] * inv
    return jnp.cos(ang), jnp.sin(ang)


def rotary(t, cos, sin):
    half = t.shape[-1] // 2
    t1, t2 = t[..., :half], t[..., half:]
    cos = cos.astype(t.dtype)
    sin = sin.astype(t.dtype)
    return jnp.concatenate([t1 * cos - t2 * sin, t2 * cos + t1 * sin], axis=-1)


def conv_module(x, w_in, b_in, w_dw, b_dw, ln_g, ln_b, w_out, b_out):
    h = x @ w_in + b_in
    a, gate = jnp.split(h, 2, axis=-1)
    h = a * jax.nn.sigmoid(gate)
    h = lax.conv_general_dilated(
        h, w_dw[:, None, :], window_strides=(1,), padding=[(CONV_WIDTH - 1, 0)],
        dimension_numbers=('NWC', 'WIO', 'NWC'), feature_group_count=D_MODEL) + b_dw
    h = jax.nn.silu(layer_norm(h, ln_g, ln_b))
    return h @ w_out + b_out


def mla_mixer(x, cos, sin, w_in, q_norm, kv_norm, w_uq, w_uk, w_uv, w_out):
    B, S, _ = x.shape
    c = x @ w_in
    c_q = rms_norm(c[..., :MLA_Q_RANK], q_norm)
    c_kv = rms_norm(c[..., MLA_Q_RANK:MLA_Q_RANK + MLA_KV_RANK], kv_norm)
    k_rope = rotary(c[..., MLA_Q_RANK + MLA_KV_RANK:], cos, sin)
    q = (c_q @ w_uq).reshape(B, S, MLA_HEADS, MLA_NOPE + MLA_ROPE)
    q_nope = q[..., :MLA_NOPE]
    q_rope = rotary(q[..., MLA_NOPE:], cos[:, :, None, :], sin[:, :, None, :])
    k_nope = (c_kv @ w_uk).reshape(B, S, MLA_HEADS, MLA_NOPE)
    v = (c_kv @ w_uv).reshape(B, S, MLA_HEADS, MLA_V)
    scale = (MLA_NOPE + MLA_ROPE) ** -0.5
    nb = S // Q_BLOCK
    key_chunk = jnp.arange(S) // CHUNK

    def to_blocks(t):
        return t.reshape(B, nb, Q_BLOCK, *t.shape[2:]).swapaxes(0, 1)

    def attend(args):
        qn, qr, start = args
        s = (jnp.einsum('bqhd,bkhd->bhqk', qn, k_nope)
             + jnp.einsum('bqhr,bkr->bhqk', qr, k_rope)).astype(jnp.float32) * scale
        q_chunk = (start + jnp.arange(Q_BLOCK)) // CHUNK
        mask = key_chunk[None, :] <= q_chunk[:, None]
        s = jnp.where(mask, s, jnp.finfo(jnp.float32).min)
        p = jax.nn.softmax(s, axis=-1).astype(v.dtype)
        return jnp.einsum('bhqk,bkhd->bqhd', p, v)

    starts = jnp.arange(nb, dtype=jnp.int32) * Q_BLOCK
    o = lax.map(attend, (to_blocks(q_nope), to_blocks(q_rope), starts))
    o = o.swapaxes(0, 1).reshape(B, S, MLA_HEADS * MLA_V)
    return o @ w_out


def peer(x, w_q, sub_keys, u, v):
    B, S, D = x.shape
    T = B * S
    xt = x.reshape(T, D)
    q = (xt @ w_q).reshape(T, PEER_HEADS, 2, PEER_D_QUERY // 2)
    s = jnp.einsum('thpd,pnd->thpn', q, sub_keys).astype(jnp.float32)
    sv, si = lax.top_k(s, PEER_TOPK)
    cand = sv[:, :, 0, :, None] + sv[:, :, 1, None, :]
    cv, ci = lax.top_k(cand.reshape(T, PEER_HEADS, PEER_TOPK * PEER_TOPK), PEER_TOPK)
    i1 = jnp.take_along_axis(si[:, :, 0, :], ci // PEER_TOPK, axis=-1)
    i2 = jnp.take_along_axis(si[:, :, 1, :], ci % PEER_TOPK, axis=-1)
    experts = (i1 * PEER_N_KEYS + i2).reshape(T, PEER_HEADS * PEER_TOPK)
    gates = jax.nn.softmax(cv, axis=-1).reshape(T, PEER_HEADS * PEER_TOPK).astype(x.dtype)
    nb = T // PEER_TOKEN_BLOCK

    def expert_block(args):
        xb, eb, gb = args
        ub = jnp.take(u, eb, axis=0)
        a = jax.nn.gelu(jnp.einsum('td,tkd->tk', xb, ub), approximate=False) * gb
        vb = jnp.take(v, eb, axis=0)
        return jnp.einsum('tk,tkd->td', a, vb)

    y = lax.map(expert_block, (xt.reshape(nb, PEER_TOKEN_BLOCK, D),
                               experts.reshape(nb, PEER_TOKEN_BLOCK, -1),
                               gates.reshape(nb, PEER_TOKEN_BLOCK, -1)))
    return y.reshape(B, S, D)


def setup_inputs(seed: int = 0) -> dict:
    key = jax.random.key(seed)
    ks = jax.random.split(key, 32)
    f32 = jnp.float32
    D, L, Lc, Lm = D_MODEL, DEPTH, N_CONV_LAYERS, N_MLA_LAYERS
    nrm = lambda k, shape, s: jax.random.normal(k, shape, f32) * s
    gain = lambda k, shape: 1.0 + 0.02 * jax.random.normal(k, shape, f32)
    x = jax.random.normal(ks[0], (BATCH, SEQ, D), f32)
    offsets = jax.random.randint(ks[1], (BATCH, 1), 0, 4096, dtype=jnp.int32)
    positions = offsets + jnp.arange(SEQ, dtype=jnp.int32)[None, :]
    return {
        'x': x,
        'positions': positions,
        'conv_w_in': nrm(ks[2], (Lc, D, 2 * D), D ** -0.5),
        'conv_b_in': nrm(ks[3], (Lc, 2 * D), 0.02),
        'conv_w_dw': nrm(ks[4], (Lc, CONV_WIDTH, D), CONV_WIDTH ** -0.5),
        'conv_b_dw': nrm(ks[5], (Lc, D), 0.02),
        'conv_ln_g': gain(ks[6], (Lc, D)),
        'conv_ln_b': nrm(ks[7], (Lc, D), 0.02),
        'conv_w_out': nrm(ks[8], (Lc, D, D), D ** -0.5 * DEEPNORM_BETA),
        'conv_b_out': nrm(ks[9], (Lc, D), 0.02),
        'mla_w_in': nrm(ks[10], (Lm, D, MLA_Q_RANK + MLA_KV_RANK + MLA_ROPE), D ** -0.5),
        'mla_q_norm': gain(ks[11], (Lm, MLA_Q_RANK)),
        'mla_kv_norm': gain(ks[12], (Lm, MLA_KV_RANK)),
        'mla_w_uq': nrm(ks[13], (Lm, MLA_Q_RANK, MLA_HEADS * (MLA_NOPE + MLA_ROPE)), MLA_Q_RANK ** -0.5),
        'mla_w_uk': nrm(ks[14], (Lm, MLA_KV_RANK, MLA_HEADS * MLA_NOPE), MLA_KV_RANK ** -0.5),
        'mla_w_uv': nrm(ks[15], (Lm, MLA_KV_RANK, MLA_HEADS * MLA_V), MLA_KV_RANK ** -0.5 * DEEPNORM_BETA),
        'mla_w_out': nrm(ks[16], (Lm, MLA_HEADS * MLA_V, D), (MLA_HEADS * MLA_V) ** -0.5 * DEEPNORM_BETA),
        'peer_w_q': nrm(ks[17], (L, D, PEER_HEADS * PEER_D_QUERY), D ** -0.5),
        'peer_sub_keys': nrm(ks[18], (L, 2, PEER_N_KEYS, PEER_D_QUERY // 2), (PEER_D_QUERY // 2) ** -0.5),
        'peer_u': nrm(ks[19], (L, PEER_N_EXPERTS, D), D ** -0.5),
        'peer_v': nrm(ks[20], (L, PEER_N_EXPERTS, D), DEEPNORM_BETA),
        'ln1_g': gain(ks[21], (L, D)),
        'ln1_b': nrm(ks[22], (L, D), 0.02),
        'ln2_g': gain(ks[23], (L, D)),
        'ln2_b': nrm(ks[24], (L, D), 0.02),
    }


def reference(x, positions, conv_w_in, conv_b_in, conv_w_dw, conv_b_dw, conv_ln_g, conv_ln_b,
              conv_w_out, conv_b_out, mla_w_in, mla_q_norm, mla_kv_norm, mla_w_uq, mla_w_uk,
              mla_w_uv, mla_w_out, peer_w_q, peer_sub_keys, peer_u, peer_v,
              ln1_g, ln1_b, ln2_g, ln2_b):
    cos, sin = rope_tables(positions)
    for i in range(DEPTH):
        j = i // N_MIXERS
        if i % N_MIXERS == 0:
            mix = conv_module(x, conv_w_in[j], conv_b_in[j], conv_w_dw[j], conv_b_dw[j],
                              conv_ln_g[j], conv_ln_b[j], conv_w_out[j], conv_b_out[j])
        else:
            mix = mla_mixer(x, cos, sin, mla_w_in[j], mla_q_norm[j], mla_kv_norm[j],
                            mla_w_uq[j], mla_w_uk[j], mla_w_uv[j], mla_w_out[j])
        x = layer_norm(DEEPNORM_ALPHA * x + mix, ln1_g[i], ln1_b[i])
        ff = peer(x, peer_w_q[i], peer_sub_keys[i], peer_u[i], peer_v[i])
        x = layer_norm(DEEPNORM_ALPHA * x + ff, ln2_g[i], ln2_b[i])
    return x
```

```python
import functools

import jax
import jax.numpy as jnp
from jax import lax
from jax.experimental import pallas as pl
from jax.experimental.pallas import tpu as pltpu

F32 = jnp.float32
BF16 = jnp.bfloat16

LANES = 128
VMEM_LIMIT = 56 * 1024 * 1024

D_MODEL = 1024
CHUNK = 64
CONV_WIDTH = 31
CONV_HALO = 32

MLA_HEADS = 8
MLA_NOPE = 128
MLA_ROPE = 64
MLA_V = 128
MLA_Q_RANK = 384
MLA_KV_RANK = 256
ROPE_THETA = 10000.0
QK_PAD = 256

PEER_HEADS = 8
PEER_N_KEYS = 128
PEER_D_QUERY = 256
PEER_TOPK = 16

LN_EPS = 1e-5
RMS_EPS = 1e-6

ROW_BLOCK = 512
EXPERT_TILE = 1024

NT_DIMS = (((1,), (1,)), ((), ()))


def _params(n_axes):
    return pltpu.CompilerParams(
        dimension_semantics=("arbitrary",) * n_axes, vmem_limit_bytes=VMEM_LIMIT)


def _layer_norm(x, g, b):
    mu = jnp.mean(x, axis=-1, keepdims=True)
    xc = x - mu
    var = jnp.mean(xc * xc, axis=-1, keepdims=True)
    return xc * lax.rsqrt(var + LN_EPS) * g + b


def _rms_norm(x, g):
    return x * lax.rsqrt(jnp.mean(x * x, axis=-1, keepdims=True) + RMS_EPS) * g


def _full(shape):
    return pl.BlockSpec(shape, lambda *_: (0,) * len(shape))


def _conv_layer_kernel(x_ref, w_in_ref, b_in_ref, w_dw_ref, b_dw_ref, lng_ref, lnb_ref,
                       w_out_ref, b_out_ref, l1g_ref, l1b_ref, o_ref, gbuf, *,
                       blocks_per_seq, alpha):
    tm, d = x_ref.shape
    i = pl.program_id(0)

    @pl.when(i % blocks_per_seq == 0)
    def _():
        gbuf[0:CONV_HALO, :] = jnp.zeros((CONV_HALO, d), F32)

    x = x_ref[...]
    h = jnp.dot(x.astype(BF16), w_in_ref[...], preferred_element_type=F32) + b_in_ref[...]
    gbuf[CONV_HALO:CONV_HALO + tm, :] = h[:, :d] * jax.nn.sigmoid(h[:, d:])

    base = CONV_HALO - (CONV_WIDTH - 1)
    acc = jnp.broadcast_to(b_dw_ref[...], (tm, d))
    for k in range(CONV_WIDTH):
        acc = acc + w_dw_ref[k:k + 1, :] * gbuf[base + k:base + k + tm, :]
    gbuf[0:CONV_HALO, :] = gbuf[tm:tm + CONV_HALO, :]

    y = _layer_norm(acc, lng_ref[...], lnb_ref[...])
    y = y * jax.nn.sigmoid(y)
    mix = jnp.dot(y.astype(BF16), w_out_ref[...], preferred_element_type=F32) + b_out_ref[...]
    o_ref[...] = _layer_norm(alpha * x + mix, l1g_ref[...], l1b_ref[...])


def _conv_layer(x2d, seq, w_in, b_in, w_dw, b_dw, ln_g, ln_b, w_out, b_out, l1g, l1b, alpha):
    t, d = x2d.shape
    tm = min(ROW_BLOCK, seq)
    row = lambda v: v.reshape(1, -1)
    return pl.pallas_call(
        functools.partial(_conv_layer_kernel, blocks_per_seq=seq // tm, alpha=alpha),
        grid=(t // tm,),
        in_specs=[
            pl.BlockSpec((tm, d), lambda i: (i, 0)),
            _full((d, 2 * d)), _full((1, 2 * d)), _full((CONV_WIDTH, d)), _full((1, d)),
            _full((1, d)), _full((1, d)), _full((d, d)), _full((1, d)), _full((1, d)),
            _full((1, d)),
        ],
        out_specs=pl.BlockSpec((tm, d), lambda i: (i, 0)),
        out_shape=jax.ShapeDtypeStruct((t, d), F32),
        scratch_shapes=[pltpu.VMEM((CONV_HALO + tm, d), F32)],
        compiler_params=_params(1),
        name="conv_layer",
    )(x2d, w_in.astype(BF16), row(b_in), w_dw, row(b_dw), row(ln_g), row(ln_b),
      w_out.astype(BF16), row(b_out), row(l1g), row(l1b))


def _mla_proj_kernel(x_ref, ang_ref, w_in_ref, qn_ref, kvn_ref, w_uq_ref, w_uk_ref, w_uv_ref,
                     q_ref, k_ref, v_ref, *, scale):
    x = x_ref[...].astype(BF16)
    c = jnp.dot(x, w_in_ref[...], preferred_element_type=F32)
    cq = _rms_norm(c[:, :MLA_Q_RANK], qn_ref[...])
    kv_end = MLA_Q_RANK + MLA_KV_RANK
    ckv = _rms_norm(c[:, MLA_Q_RANK:kv_end], kvn_ref[...])
    ang = ang_ref[...]
    cos, sin = jnp.cos(ang), jnp.sin(ang)
    k_rope = c[:, kv_end:kv_end + LANES] * cos + c[:, kv_end + LANES:kv_end + 2 * LANES] * sin

    q = jnp.dot(cq.astype(BF16), w_uq_ref[...], preferred_element_type=F32)
    ckv_b = ckv.astype(BF16)
    kn = jnp.dot(ckv_b, w_uk_ref[...], preferred_element_type=F32)
    v_ref[...] = jnp.dot(ckv_b, w_uv_ref[...], preferred_element_type=F32).astype(BF16)
    k_rope_b = k_rope.astype(BF16)
    for h in range(MLA_HEADS):
        qh = q[:, h * 3 * LANES:(h + 1) * 3 * LANES]
        q_rope = qh[:, LANES:2 * LANES] * cos + qh[:, 2 * LANES:] * sin
        q_ref[:, h * QK_PAD:h * QK_PAD + LANES] = (qh[:, :LANES] * scale).astype(BF16)
        q_ref[:, h * QK_PAD + LANES:(h + 1) * QK_PAD] = (q_rope * scale).astype(BF16)
        k_ref[:, h * QK_PAD:h * QK_PAD + LANES] = kn[:, h * MLA_NOPE:(h + 1) * MLA_NOPE].astype(BF16)
        k_ref[:, h * QK_PAD + LANES:(h + 1) * QK_PAD] = k_rope_b


def _half_rotation(w):
    half = w.shape[-1] // 2
    return jnp.concatenate([-w[..., half:], w[..., :half]], axis=-1)


def _pad_lanes(w):
    return jnp.pad(w, [(0, 0)] * (w.ndim - 1) + [(0, LANES - w.shape[-1])])


def _mla_proj(x2d, ang, w_in, q_norm, kv_norm, w_uq, w_uk, w_uv):
    t, d = x2d.shape
    tm = min(ROW_BLOCK, t)
    kv_end = MLA_Q_RANK + MLA_KV_RANK
    w_rope = w_in[:, kv_end:]
    w_in_ext = jnp.concatenate(
        [w_in[:, :kv_end], _pad_lanes(w_rope), _pad_lanes(_half_rotation(w_rope))], axis=1).astype(BF16)
    wq = w_uq.reshape(MLA_Q_RANK, MLA_HEADS, MLA_NOPE + MLA_ROPE)
    wq_rope = wq[..., MLA_NOPE:]
    w_uq_ext = jnp.concatenate(
        [wq[..., :MLA_NOPE], _pad_lanes(wq_rope), _pad_lanes(_half_rotation(wq_rope))],
        axis=-1).reshape(MLA_Q_RANK, MLA_HEADS * 3 * LANES).astype(BF16)
    scale = (MLA_NOPE + MLA_ROPE) ** -0.5
    n_in = w_in_ext.shape[1]
    return pl.pallas_call(
        functools.partial(_mla_proj_kernel, scale=scale),
        grid=(t // tm,),
        in_specs=[
            pl.BlockSpec((tm, d), lambda i: (i, 0)),
            pl.BlockSpec((tm, LANES), lambda i: (i, 0)),
            _full((d, n_in)), _full((1, MLA_Q_RANK)), _full((1, MLA_KV_RANK)),
            _full((MLA_Q_RANK, MLA_HEADS * 3 * LANES)),
            _full((MLA_KV_RANK, MLA_HEADS * MLA_NOPE)), _full((MLA_KV_RANK, MLA_HEADS * MLA_V)),
        ],
        out_specs=[
            pl.BlockSpec((tm, MLA_HEADS * QK_PAD), lambda i: (i, 0)),
            pl.BlockSpec((tm, MLA_HEADS * QK_PAD), lambda i: (i, 0)),
            pl.BlockSpec((tm, MLA_HEADS * MLA_V), lambda i: (i, 0)),
        ],
        out_shape=[
            jax.ShapeDtypeStruct((t, MLA_HEADS * QK_PAD), BF16),
            jax.ShapeDtypeStruct((t, MLA_HEADS * QK_PAD), BF16),
            jax.ShapeDtypeStruct((t, MLA_HEADS * MLA_V), BF16),
        ],
        compiler_params=_params(1),
        name="mla_proj",
    )(x2d, ang, w_in_ext, q_norm.reshape(1, -1), kv_norm.reshape(1, -1), w_uq_ext,
      w_uk.astype(BF16), w_uv.astype(BF16))


def _attn_kernel(q_ref, k_ref, v_ref, o_ref):
    tq = q_ref.shape[0]
    qi = pl.program_id(2)
    q = q_ref[...]

    def step(k, v, carry, mask):
        m, l, acc = carry
        s = lax.dot_general(q, k, NT_DIMS, preferred_element_type=F32)
        if mask is not None:
            s = jnp.where(mask, s, jnp.finfo(F32).min)
        m_new = jnp.maximum(m, jnp.max(s, axis=-1, keepdims=True))
        corr = jnp.exp(m - m_new)
        p = jnp.exp(s - m_new)
        l = corr * l + jnp.sum(p, axis=-1, keepdims=True)
        acc = corr * acc + jnp.dot(p.astype(BF16), v, preferred_element_type=F32)
        return m_new, l, acc

    def body(kb, carry):
        r0 = pl.multiple_of(kb * tq, tq)
        return step(k_ref[pl.ds(r0, tq), :], v_ref[pl.ds(r0, tq), :], carry, None)

    init = (jnp.full((tq, 1), -jnp.inf, F32), jnp.zeros((tq, 1), F32), jnp.zeros((tq, MLA_V), F32))
    carry = lax.fori_loop(0, qi, body, init)
    chunk_bits = CHUNK.bit_length() - 1
    q_chunk = lax.shift_right_logical(lax.broadcasted_iota(jnp.int32, (tq, tq), 0), chunk_bits)
    k_chunk = lax.shift_right_logical(lax.broadcasted_iota(jnp.int32, (tq, tq), 1), chunk_bits)
    r0 = pl.multiple_of(qi * tq, tq)
    _, l, acc = step(k_ref[pl.ds(r0, tq), :], v_ref[pl.ds(r0, tq), :], carry, k_chunk <= q_chunk)
    o_ref[...] = (acc / l).astype(BF16)


def _attention(q, k, v, batch, seq):
    t = q.shape[0]
    tq = min(ROW_BLOCK, seq)
    nq = seq // tq
    return pl.pallas_call(
        _attn_kernel,
        grid=(batch, MLA_HEADS, nq),
        in_specs=[
            pl.BlockSpec((tq, QK_PAD), lambda b, h, i: (b * nq + i, h)),
            pl.BlockSpec((seq, QK_PAD), lambda b, h, i: (b, h)),
            pl.BlockSpec((seq, MLA_V), lambda b, h, i: (b, h)),
        ],
        out_specs=pl.BlockSpec((tq, MLA_V), lambda b, h, i: (b * nq + i, h)),
        out_shape=jax.ShapeDtypeStruct((t, MLA_HEADS * MLA_V), BF16),
        compiler_params=_params(3),
        name="mla_attention",
    )(q, k, v)


def _out_proj_kernel(o_ref, x_ref, w_ref, g_ref, b_ref, y_ref, *, alpha):
    mix = jnp.dot(o_ref[...], w_ref[...], preferred_element_type=F32)
    y_ref[...] = _layer_norm(alpha * x_ref[...] + mix, g_ref[...], b_ref[...])


def _out_proj(o, x2d, w_out, g, b, alpha):
    t, d = x2d.shape
    tm = min(ROW_BLOCK, t)
    return pl.pallas_call(
        functools.partial(_out_proj_kernel, alpha=alpha),
        grid=(t // tm,),
        in_specs=[
            pl.BlockSpec((tm, o.shape[1]), lambda i: (i, 0)),
            pl.BlockSpec((tm, d), lambda i: (i, 0)),
            _full(w_out.shape), _full((1, d)), _full((1, d)),
        ],
        out_specs=pl.BlockSpec((tm, d), lambda i: (i, 0)),
        out_shape=jax.ShapeDtypeStruct((t, d), F32),
        compiler_params=_params(1),
        name="mla_out_proj",
    )(o, x2d, w_out.astype(BF16), g.reshape(1, -1), b.reshape(1, -1))


def _mla_layer(x2d, ang, batch, seq, w_in, q_norm, kv_norm, w_uq, w_uk, w_uv, w_out, l1g, l1b, alpha):
    q, k, v = _mla_proj(x2d, ang, w_in, q_norm, kv_norm, w_uq, w_uk, w_uv)
    o = _attention(q, k, v, batch, seq)
    return _out_proj(o, x2d, w_out, l1g, l1b, alpha)


def _top_k_ranked(s):
    n = s.shape[0]
    key_id = lax.broadcasted_iota(jnp.int32, s.shape, 0)
    rank = jnp.full(s.shape, float(PEER_TOPK), F32)
    vals = []
    for r in range(PEER_TOPK):
        m = jnp.max(s, axis=0, keepdims=True)
        first = jnp.min(jnp.where(s == m, key_id, n), axis=0, keepdims=True)
        hit = key_id == first
        rank = jnp.where(hit, float(r), rank)
        s = jnp.where(hit, -jnp.inf, s)
        vals.append(m)
    return jnp.concatenate(vals, axis=0), rank


def _merge_top_k(sv1, sv2):
    row_id = lax.broadcasted_iota(jnp.int32, sv1.shape, 0).astype(F32)
    count = jnp.zeros(sv1.shape, F32)
    front = sv1 + sv2[0:1]
    picked = []
    for r in range(PEER_TOPK):
        m = jnp.max(front, axis=0, keepdims=True)
        picked.append(m)
        first = jnp.min(jnp.where(front == m, row_id, float(PEER_TOPK)), axis=0, keepdims=True)
        hit = row_id == first
        count = jnp.where(hit, count + 1.0, count)
        if r + 1 < PEER_TOPK:
            new_len = jnp.sum(jnp.where(hit, count, 0.0), axis=0, keepdims=True)
            a_val = jnp.sum(jnp.where(hit, sv1, 0.0), axis=0, keepdims=True)
            b_val = jnp.sum(jnp.where(row_id == new_len, sv2, 0.0), axis=0, keepdims=True)
            nxt = jnp.where(new_len >= float(PEER_TOPK), -jnp.inf, a_val + b_val)
            front = jnp.where(hit, nxt, front)
    return count, picked


def _route_kernel(x_ref, wq_ref, keys_ref, rank2_ref, e2_ref, len1_ref, e1_ref, s_scr):
    tb = x_ref.shape[0]
    n_chunks = tb // LANES
    half = PEER_D_QUERY // 2
    xb = x_ref[...].astype(BF16)
    q_t = lax.dot_general(wq_ref[...], xb, NT_DIMS, preferred_element_type=F32)
    for h in range(PEER_HEADS):
        for p in range(2):
            r0 = h * PEER_D_QUERY + p * half
            s = jnp.dot(keys_ref[p], q_t[r0:r0 + half, :].astype(BF16), preferred_element_type=F32)
            for c in range(n_chunks):
                s_scr[2 * h + p, c] = s[:, c * LANES:(c + 1) * LANES]

    def body(idx, carry):
        h = idx // n_chunks
        c = idx % n_chunks
        s1 = s_scr[2 * h, c]
        s2 = s_scr[2 * h + 1, c]
        sv1, rank1 = _top_k_ranked(s1)
        sv2, rank2 = _top_k_ranked(s2)
        count, picked = _merge_top_k(sv1, sv2)
        z = jnp.ones_like(picked[0])
        for r in range(1, PEER_TOPK):
            z = z + jnp.exp(picked[r] - picked[0])
        len1 = jnp.zeros(s1.shape, F32)
        for a in range(PEER_TOPK):
            len1 = jnp.where(rank1 == float(a), count[a:a + 1], len1)
        rank2_ref[h, c] = rank2
        e2_ref[h, c] = jnp.exp(s2 - sv2[0:1])
        len1_ref[h, c] = len1
        e1_ref[h, c] = jnp.exp(s1 - sv1[0:1]) / z
        return carry

    lax.fori_loop(0, PEER_HEADS * n_chunks, body, 0)


def _route(x2d, w_q, sub_keys):
    t, d = x2d.shape
    tb = min(ROW_BLOCK, t)
    nc = tb // LANES
    table = jax.ShapeDtypeStruct((PEER_HEADS, t // LANES, PEER_N_KEYS, LANES), F32)
    table_spec = pl.BlockSpec((PEER_HEADS, nc, PEER_N_KEYS, LANES), lambda i: (0, i, 0, 0))
    return pl.pallas_call(
        _route_kernel,
        grid=(t // tb,),
        in_specs=[
            pl.BlockSpec((tb, d), lambda i: (i, 0)),
            _full((PEER_HEADS * PEER_D_QUERY, d)),
            _full((2, PEER_N_KEYS, PEER_D_QUERY // 2)),
        ],
        out_specs=[table_spec] * 4,
        out_shape=[table] * 4,
        scratch_shapes=[pltpu.VMEM((2 * PEER_HEADS, nc, PEER_N_KEYS, LANES), F32)],
        compiler_params=_params(1),
        name="peer_route",
    )(x2d, w_q.T.astype(BF16), sub_keys.astype(BF16))


def _expert_kernel(x_ref, u_ref, vt_ref, rank2_ref, e2_ref, len1_ref, e1_ref, g_ref, b_ref,
                   o_ref, xb_scr, h_scr, a_scr, acc_scr, *, alpha):
    tb = x_ref.shape[0]
    te = u_ref.shape[0]
    n_chunks = tb // LANES
    n_slabs = te // PEER_N_KEYS
    j = pl.program_id(1)

    @pl.when(j == 0)
    def _():
        xb_scr[...] = x_ref[...].astype(BF16)
        acc_scr[...] = jnp.zeros(acc_scr.shape, F32)

    h_scr[...] = lax.dot_general(u_ref[...], xb_scr[...], NT_DIMS, preferred_element_type=F32)

    def slab(s, carry):
        i1 = j * n_slabs + s
        r0 = pl.multiple_of(s * PEER_N_KEYS, PEER_N_KEYS)
        for c in range(n_chunks):
            gate = jnp.zeros((PEER_N_KEYS, LANES), F32)
            for h in range(PEER_HEADS):
                length = len1_ref[h, c, pl.ds(i1, 1), :]
                e1 = e1_ref[h, c, pl.ds(i1, 1), :]
                gate = gate + jnp.where(rank2_ref[h, c] < length, e2_ref[h, c], 0.0) * e1
            pre = h_scr[pl.ds(r0, PEER_N_KEYS), c * LANES:(c + 1) * LANES]
            act = 0.5 * pre * (1.0 + lax.erf(pre * (2.0 ** -0.5)))
            a_scr[pl.ds(r0, PEER_N_KEYS), c * LANES:(c + 1) * LANES] = (act * gate).astype(BF16)
        return carry

    lax.fori_loop(0, n_slabs, slab, 0)
    acc_scr[...] += jnp.dot(vt_ref[...], a_scr[...], preferred_element_type=F32)

    @pl.when(j == pl.num_programs(1) - 1)
    def _():
        y = acc_scr[...].T
        o_ref[...] = _layer_norm(alpha * x_ref[...] + y, g_ref[...], b_ref[...])


def _experts(x2d, tables, u, v, g, b, alpha):
    t, d = x2d.shape
    n_experts = u.shape[0]
    tb = min(ROW_BLOCK, t)
    te = EXPERT_TILE
    nc = tb // LANES
    table_spec = pl.BlockSpec((PEER_HEADS, nc, PEER_N_KEYS, LANES), lambda i, j: (0, i, 0, 0))
    return pl.pallas_call(
        functools.partial(_expert_kernel, alpha=alpha),
        grid=(t // tb, n_experts // te),
        in_specs=[
            pl.BlockSpec((tb, d), lambda i, j: (i, 0)),
            pl.BlockSpec((te, d), lambda i, j: (j, 0)),
            pl.BlockSpec((d, te), lambda i, j: (0, j)),
            table_spec, table_spec, table_spec, table_spec,
            pl.BlockSpec((1, d), lambda i, j: (0, 0)),
            pl.BlockSpec((1, d), lambda i, j: (0, 0)),
        ],
        out_specs=pl.BlockSpec((tb, d), lambda i, j: (i, 0)),
        out_shape=jax.ShapeDtypeStruct((t, d), F32),
        scratch_shapes=[
            pltpu.VMEM((tb, d), BF16),
            pltpu.VMEM((te, tb), F32),
            pltpu.VMEM((te, tb), BF16),
            pltpu.VMEM((d, tb), F32),
        ],
        compiler_params=_params(2),
        name="peer_experts",
    )(x2d, u.astype(BF16), v.T.astype(BF16), *tables, g.reshape(1, -1), b.reshape(1, -1))


def _peer_layer(x2d, w_q, sub_keys, u, v, g, b, alpha):
    tables = _route(x2d, w_q, sub_keys)
    return _experts(x2d, tables, u, v, g, b, alpha)


def kernel(x, positions, conv_w_in, conv_b_in, conv_w_dw, conv_b_dw, conv_ln_g, conv_ln_b,
           conv_w_out, conv_b_out, mla_w_in, mla_q_norm, mla_kv_norm, mla_w_uq, mla_w_uk,
           mla_w_uv, mla_w_out, peer_w_q, peer_sub_keys, peer_u, peer_v,
           ln1_g, ln1_b, ln2_g, ln2_b):
    batch, seq, d = x.shape
    depth = ln1_g.shape[0]
    alpha = (2 * depth) ** 0.25
    x2d = x.reshape(batch * seq, d)

    inv = 1.0 / (ROPE_THETA ** (jnp.arange(0, MLA_ROPE, 2, dtype=F32) / MLA_ROPE))
    ang = positions.astype(F32).reshape(batch * seq, 1) * inv
    ang = _pad_lanes(jnp.concatenate([ang, ang], axis=-1))

    for i in range(depth):
        j = i // 2
        if i % 2 == 0:
            x2d = _conv_layer(x2d, seq, conv_w_in[j], conv_b_in[j], conv_w_dw[j], conv_b_dw[j],
                              conv_ln_g[j], conv_ln_b[j], conv_w_out[j], conv_b_out[j],
                              ln1_g[i], ln1_b[i], alpha)
        else:
            x2d = _mla_layer(x2d, ang, batch, seq, mla_w_in[j], mla_q_norm[j], mla_kv_norm[j],
                             mla_w_uq[j], mla_w_uk[j], mla_w_uv[j], mla_w_out[j],
                             ln1_g[i], ln1_b[i], alpha)
        x2d = _peer_layer(x2d, peer_w_q[i], peer_sub_keys[i], peer_u[i], peer_v[i],
                          ln2_g[i], ln2_b[i], alpha)
    return x2d.reshape(batch, seq, d)
```

```python
import functools

import jax
import jax.numpy as jnp
from jax import lax
from jax.experimental import pallas as pl
from jax.experimental.pallas import tpu as pltpu

F32 = jnp.float32
BF16 = jnp.bfloat16

LANES = 128
VMEM_LIMIT = 56 * 1024 * 1024

D_MODEL = 1024
CHUNK = 64
CONV_WIDTH = 31
CONV_HALO = 32

MLA_HEADS = 8
MLA_NOPE = 128
MLA_ROPE = 64
MLA_V = 128
MLA_Q_RANK = 384
MLA_KV_RANK = 256
ROPE_THETA = 10000.0
QK_PAD = 256

PEER_HEADS = 8
PEER_N_KEYS = 128
PEER_D_QUERY = 256
PEER_TOPK = 16

LN_EPS = 1e-5
RMS_EPS = 1e-6

ROW_BLOCK = 512
EXPERT_TILE = 2048
EXPERT_SUB_TILE = 512

NT_DIMS = (((1,), (1,)), ((), ()))


def _params(n_axes):
    return pltpu.CompilerParams(
        dimension_semantics=("arbitrary",) * n_axes, vmem_limit_bytes=VMEM_LIMIT)


def _layer_norm(x, g, b):
    mu = jnp.mean(x, axis=-1, keepdims=True)
    xc = x - mu
    var = jnp.mean(xc * xc, axis=-1, keepdims=True)
    return xc * lax.rsqrt(var + LN_EPS) * g + b


def _rms_norm(x, g):
    return x * lax.rsqrt(jnp.mean(x * x, axis=-1, keepdims=True) + RMS_EPS) * g


def _full(shape):
    return pl.BlockSpec(shape, lambda *_: (0,) * len(shape))


def _conv_layer_kernel(x_ref, w_in_ref, b_in_ref, w_dw_ref, b_dw_ref, lng_ref, lnb_ref,
                       w_out_ref, b_out_ref, l1g_ref, l1b_ref, o_ref, gbuf, *,
                       blocks_per_seq, alpha):
    tm, d = x_ref.shape
    i = pl.program_id(0)

    @pl.when(i % blocks_per_seq == 0)
    def _():
        gbuf[0:CONV_HALO, :] = jnp.zeros((CONV_HALO, d), F32)

    x = x_ref[...]
    h = jnp.dot(x.astype(BF16), w_in_ref[...], preferred_element_type=F32) + b_in_ref[...]
    gbuf[CONV_HALO:CONV_HALO + tm, :] = h[:, :d] * jax.nn.sigmoid(h[:, d:])

    base = CONV_HALO - (CONV_WIDTH - 1)
    acc = jnp.broadcast_to(b_dw_ref[...], (tm, d))
    for k in range(CONV_WIDTH):
        acc = acc + w_dw_ref[k:k + 1, :] * gbuf[base + k:base + k + tm, :]
    gbuf[0:CONV_HALO, :] = gbuf[tm:tm + CONV_HALO, :]

    y = _layer_norm(acc, lng_ref[...], lnb_ref[...])
    y = y * jax.nn.sigmoid(y)
    mix = jnp.dot(y.astype(BF16), w_out_ref[...], preferred_element_type=F32) + b_out_ref[...]
    o_ref[...] = _layer_norm(alpha * x + mix, l1g_ref[...], l1b_ref[...])


def _conv_layer(x2d, seq, w_in, b_in, w_dw, b_dw, ln_g, ln_b, w_out, b_out, l1g, l1b, alpha):
    t, d = x2d.shape
    tm = min(ROW_BLOCK, seq)
    row = lambda v: v.reshape(1, -1)
    return pl.pallas_call(
        functools.partial(_conv_layer_kernel, blocks_per_seq=seq // tm, alpha=alpha),
        grid=(t // tm,),
        in_specs=[
            pl.BlockSpec((tm, d), lambda i: (i, 0)),
            _full((d, 2 * d)), _full((1, 2 * d)), _full((CONV_WIDTH, d)), _full((1, d)),
            _full((1, d)), _full((1, d)), _full((d, d)), _full((1, d)), _full((1, d)),
            _full((1, d)),
        ],
        out_specs=pl.BlockSpec((tm, d), lambda i: (i, 0)),
        out_shape=jax.ShapeDtypeStruct((t, d), F32),
        scratch_shapes=[pltpu.VMEM((CONV_HALO + tm, d), F32)],
        compiler_params=_params(1),
        name="conv_layer",
    )(x2d, w_in.astype(BF16), row(b_in), w_dw, row(b_dw), row(ln_g), row(ln_b),
      w_out.astype(BF16), row(b_out), row(l1g), row(l1b))


def _mla_proj_kernel(x_ref, ang_ref, w_in_ref, qn_ref, kvn_ref, w_uq_ref, w_uk_ref, w_uv_ref,
                     q_ref, k_ref, v_ref, *, scale):
    x = x_ref[...].astype(BF16)
    c = jnp.dot(x, w_in_ref[...], preferred_element_type=F32)
    cq = _rms_norm(c[:, :MLA_Q_RANK], qn_ref[...])
    kv_end = MLA_Q_RANK + MLA_KV_RANK
    ckv = _rms_norm(c[:, MLA_Q_RANK:kv_end], kvn_ref[...])
    ang = ang_ref[...]
    cos, sin = jnp.cos(ang), jnp.sin(ang)
    k_rope = c[:, kv_end:kv_end + LANES] * cos + c[:, kv_end + LANES:kv_end + 2 * LANES] * sin

    q = jnp.dot(cq.astype(BF16), w_uq_ref[...], preferred_element_type=F32)
    ckv_b = ckv.astype(BF16)
    kn = jnp.dot(ckv_b, w_uk_ref[...], preferred_element_type=F32)
    v_ref[...] = jnp.dot(ckv_b, w_uv_ref[...], preferred_element_type=F32).astype(BF16)
    k_rope_b = k_rope.astype(BF16)
    for h in range(MLA_HEADS):
        qh = q[:, h * 3 * LANES:(h + 1) * 3 * LANES]
        q_rope = qh[:, LANES:2 * LANES] * cos + qh[:, 2 * LANES:] * sin
        q_ref[:, h * QK_PAD:h * QK_PAD + LANES] = (qh[:, :LANES] * scale).astype(BF16)
        q_ref[:, h * QK_PAD + LANES:(h + 1) * QK_PAD] = (q_rope * scale).astype(BF16)
        k_ref[:, h * QK_PAD:h * QK_PAD + LANES] = kn[:, h * MLA_NOPE:(h + 1) * MLA_NOPE].astype(BF16)
        k_ref[:, h * QK_PAD + LANES:(h + 1) * QK_PAD] = k_rope_b


def _half_rotation(w):
    half = w.shape[-1] // 2
    return jnp.concatenate([-w[..., half:], w[..., :half]], axis=-1)


def _pad_lanes(w):
    return jnp.pad(w, [(0, 0)] * (w.ndim - 1) + [(0, LANES - w.shape[-1])])


def _mla_proj(x2d, ang, w_in, q_norm, kv_norm, w_uq, w_uk, w_uv):
    t, d = x2d.shape
    tm = min(ROW_BLOCK, t)
    kv_end = MLA_Q_RANK + MLA_KV_RANK
    w_rope = w_in[:, kv_end:]
    w_in_ext = jnp.concatenate(
        [w_in[:, :kv_end], _pad_lanes(w_rope), _pad_lanes(_half_rotation(w_rope))], axis=1).astype(BF16)
    wq = w_uq.reshape(MLA_Q_RANK, MLA_HEADS, MLA_NOPE + MLA_ROPE)
    wq_rope = wq[..., MLA_NOPE:]
    w_uq_ext = jnp.concatenate(
        [wq[..., :MLA_NOPE], _pad_lanes(wq_rope), _pad_lanes(_half_rotation(wq_rope))],
        axis=-1).reshape(MLA_Q_RANK, MLA_HEADS * 3 * LANES).astype(BF16)
    scale = (MLA_NOPE + MLA_ROPE) ** -0.5
    n_in = w_in_ext.shape[1]
    return pl.pallas_call(
        functools.partial(_mla_proj_kernel, scale=scale),
        grid=(t // tm,),
        in_specs=[
            pl.BlockSpec((tm, d), lambda i: (i, 0)),
            pl.BlockSpec((tm, LANES), lambda i: (i, 0)),
            _full((d, n_in)), _full((1, MLA_Q_RANK)), _full((1, MLA_KV_RANK)),
            _full((MLA_Q_RANK, MLA_HEADS * 3 * LANES)),
            _full((MLA_KV_RANK, MLA_HEADS * MLA_NOPE)), _full((MLA_KV_RANK, MLA_HEADS * MLA_V)),
        ],
        out_specs=[
            pl.BlockSpec((tm, MLA_HEADS * QK_PAD), lambda i: (i, 0)),
            pl.BlockSpec((tm, MLA_HEADS * QK_PAD), lambda i: (i, 0)),
            pl.BlockSpec((tm, MLA_HEADS * MLA_V), lambda i: (i, 0)),
        ],
        out_shape=[
            jax.ShapeDtypeStruct((t, MLA_HEADS * QK_PAD), BF16),
            jax.ShapeDtypeStruct((t, MLA_HEADS * QK_PAD), BF16),
            jax.ShapeDtypeStruct((t, MLA_HEADS * MLA_V), BF16),
        ],
        compiler_params=_params(1),
        name="mla_proj",
    )(x2d, ang, w_in_ext, q_norm.reshape(1, -1), kv_norm.reshape(1, -1), w_uq_ext,
      w_uk.astype(BF16), w_uv.astype(BF16))


def _attn_kernel(q_ref, k_ref, v_ref, o_ref):
    tq = q_ref.shape[0]
    qi = pl.program_id(2)
    q = q_ref[...]

    def step(k, v, carry, mask):
        m, l, acc = carry
        s = lax.dot_general(q, k, NT_DIMS, preferred_element_type=F32)
        if mask is not None:
            s = jnp.where(mask, s, jnp.finfo(F32).min)
        m_new = jnp.maximum(m, jnp.max(s, axis=-1, keepdims=True))
        corr = jnp.exp(m - m_new)
        p = jnp.exp(s - m_new)
        l = corr * l + jnp.sum(p, axis=-1, keepdims=True)
        acc = corr * acc + jnp.dot(p.astype(BF16), v, preferred_element_type=F32)
        return m_new, l, acc

    def body(kb, carry):
        r0 = pl.multiple_of(kb * tq, tq)
        return step(k_ref[pl.ds(r0, tq), :], v_ref[pl.ds(r0, tq), :], carry, None)

    init = (jnp.full((tq, 1), -jnp.inf, F32), jnp.zeros((tq, 1), F32), jnp.zeros((tq, MLA_V), F32))
    carry = lax.fori_loop(0, qi, body, init)
    chunk_bits = CHUNK.bit_length() - 1
    q_chunk = lax.shift_right_logical(lax.broadcasted_iota(jnp.int32, (tq, tq), 0), chunk_bits)
    k_chunk = lax.shift_right_logical(lax.broadcasted_iota(jnp.int32, (tq, tq), 1), chunk_bits)
    r0 = pl.multiple_of(qi * tq, tq)
    _, l, acc = step(k_ref[pl.ds(r0, tq), :], v_ref[pl.ds(r0, tq), :], carry, k_chunk <= q_chunk)
    o_ref[...] = (acc / l).astype(BF16)


def _attention(q, k, v, batch, seq):
    t = q.shape[0]
    tq = min(ROW_BLOCK, seq)
    nq = seq // tq
    return pl.pallas_call(
        _attn_kernel,
        grid=(batch, MLA_HEADS, nq),
        in_specs=[
            pl.BlockSpec((tq, QK_PAD), lambda b, h, i: (b * nq + i, h)),
            pl.BlockSpec((seq, QK_PAD), lambda b, h, i: (b, h)),
            pl.BlockSpec((seq, MLA_V), lambda b, h, i: (b, h)),
        ],
        out_specs=pl.BlockSpec((tq, MLA_V), lambda b, h, i: (b * nq + i, h)),
        out_shape=jax.ShapeDtypeStruct((t, MLA_HEADS * MLA_V), BF16),
        compiler_params=_params(3),
        name="mla_attention",
    )(q, k, v)


def _out_proj_kernel(o_ref, x_ref, w_ref, g_ref, b_ref, y_ref, *, alpha):
    mix = jnp.dot(o_ref[...], w_ref[...], preferred_element_type=F32)
    y_ref[...] = _layer_norm(alpha * x_ref[...] + mix, g_ref[...], b_ref[...])


def _out_proj(o, x2d, w_out, g, b, alpha):
    t, d = x2d.shape
    tm = min(ROW_BLOCK, t)
    return pl.pallas_call(
        functools.partial(_out_proj_kernel, alpha=alpha),
        grid=(t // tm,),
        in_specs=[
            pl.BlockSpec((tm, o.shape[1]), lambda i: (i, 0)),
            pl.BlockSpec((tm, d), lambda i: (i, 0)),
            _full(w_out.shape), _full((1, d)), _full((1, d)),
        ],
        out_specs=pl.BlockSpec((tm, d), lambda i: (i, 0)),
        out_shape=jax.ShapeDtypeStruct((t, d), F32),
        compiler_params=_params(1),
        name="mla_out_proj",
    )(o, x2d, w_out.astype(BF16), g.reshape(1, -1), b.reshape(1, -1))


def _mla_layer(x2d, ang, batch, seq, w_in, q_norm, kv_norm, w_uq, w_uk, w_uv, w_out, l1g, l1b, alpha):
    q, k, v = _mla_proj(x2d, ang, w_in, q_norm, kv_norm, w_uq, w_uk, w_uv)
    o = _attention(q, k, v, batch, seq)
    return _out_proj(o, x2d, w_out, l1g, l1b, alpha)


def _top_k_ranked(s):
    n = s.shape[0]
    key_id = lax.broadcasted_iota(jnp.int32, s.shape, 0)
    rank = jnp.full(s.shape, float(PEER_TOPK), F32)
    vals = []
    for r in range(PEER_TOPK):
        m = jnp.max(s, axis=0, keepdims=True)
        first = jnp.min(jnp.where(s == m, key_id, n), axis=0, keepdims=True)
        hit = key_id == first
        rank = jnp.where(hit, float(r), rank)
        s = jnp.where(hit, -jnp.inf, s)
        vals.append(m)
    return jnp.concatenate(vals, axis=0), rank


def _merge_top_k(sv1, sv2):
    row_id = lax.broadcasted_iota(jnp.int32, sv1.shape, 0).astype(F32)
    count = jnp.zeros(sv1.shape, F32)
    front = sv1 + sv2[0:1]
    picked = []
    for r in range(PEER_TOPK):
        m = jnp.max(front, axis=0, keepdims=True)
        picked.append(m)
        first = jnp.min(jnp.where(front == m, row_id, float(PEER_TOPK)), axis=0, keepdims=True)
        hit = row_id == first
        count = jnp.where(hit, count + 1.0, count)
        if r + 1 < PEER_TOPK:
            new_len = jnp.sum(jnp.where(hit, count, 0.0), axis=0, keepdims=True)
            a_val = jnp.sum(jnp.where(hit, sv1, 0.0), axis=0, keepdims=True)
            b_val = jnp.sum(jnp.where(row_id == new_len, sv2, 0.0), axis=0, keepdims=True)
            nxt = jnp.where(new_len >= float(PEER_TOPK), -jnp.inf, a_val + b_val)
            front = jnp.where(hit, nxt, front)
    return count, picked


def _top_k_no_ties(s):
    rank = jnp.full(s.shape, float(PEER_TOPK), F32)
    vals = []
    for r in range(PEER_TOPK):
        m = jnp.max(s, axis=0, keepdims=True)
        hit = s == m
        rank = jnp.where(hit, float(r), rank)
        s = jnp.where(hit, -jnp.inf, s)
        vals.append(m)
    members = jnp.sum(jnp.where(rank < float(PEER_TOPK), 1.0, 0.0), axis=0, keepdims=True)
    return vals, rank, members


def _merge_no_ties(v1, v2):
    sub = 8
    lo2 = jnp.concatenate(v2[:sub], axis=0)
    hi2 = jnp.concatenate(v2[sub:], axis=0)
    hi1 = jnp.concatenate(v1[sub:], axis=0)
    row = lax.broadcasted_iota(jnp.int32, lo2.shape, 0)
    cand, valid = [], []
    for a in range(sub):
        lim = min(PEER_TOPK // (a + 1), sub)
        ok = row < lim
        cand.append(jnp.where(ok, v1[a] + lo2, -jnp.inf))
        valid.append(ok)
    cand.append(v1[0] + hi2)
    cand.append(hi1 + v2[0])
    valid += [None, None]
    picked = []
    for r in range(PEER_TOPK):
        m = cand[0]
        for c in cand[1:]:
            m = jnp.maximum(m, c)
        m = jnp.max(m, axis=0, keepdims=True)
        picked.append(m)
        cand = [jnp.where(c == m, -jnp.inf, c) for c in cand]
    taken = []
    for c, ok in zip(cand, valid):
        t = c == -jnp.inf
        taken.append(jnp.where(t if ok is None else t & ok, 1.0, 0.0))
    counts = [jnp.sum(t, axis=0, keepdims=True) for t in taken[:sub]]
    counts[0] = counts[0] + jnp.sum(taken[sub], axis=0, keepdims=True)
    counts += [taken[sub + 1][a:a + 1] for a in range(sub)]
    total = counts[0]
    for cnt in counts[1:]:
        total = total + cnt
    return counts, picked, total


def _route_kernel(x_ref, wq_ref, keys_ref, rank2_ref, e2_ref, len1_ref, e1_ref, s_scr):
    tb = x_ref.shape[0]
    n_chunks = tb // LANES
    half = PEER_D_QUERY // 2
    xb = x_ref[...].astype(BF16)
    q_t = lax.dot_general(wq_ref[...], xb, NT_DIMS, preferred_element_type=F32)
    for h in range(PEER_HEADS):
        for p in range(2):
            r0 = h * PEER_D_QUERY + p * half
            s = jnp.dot(keys_ref[p], q_t[r0:r0 + half, :].astype(BF16), preferred_element_type=F32)
            for c in range(n_chunks):
                s_scr[2 * h + p, c] = s[:, c * LANES:(c + 1) * LANES]

    def emit(h, c, s1, s2, rank1, rank2, top1, top2, counts, picked):
        z = jnp.ones_like(picked[0])
        for r in range(1, PEER_TOPK):
            z = z + jnp.exp(picked[r] - picked[0])
        len1 = jnp.zeros(s1.shape, F32)
        for a in range(PEER_TOPK):
            len1 = jnp.where(rank1 == float(a), counts[a], len1)
        rank2_ref[h, c] = rank2
        e2_ref[h, c] = jnp.exp(s2 - top2)
        len1_ref[h, c] = len1
        e1_ref[h, c] = jnp.exp(s1 - top1) * (0.5 / z)

    group = min(4, n_chunks)

    def body(idx, carry):
        h = idx // (n_chunks // group)
        c0 = (idx % (n_chunks // group)) * group
        flags = []
        for dc in range(group):
            c = c0 + dc
            s1 = s_scr[2 * h, c]
            s2 = s_scr[2 * h + 1, c]
            v1, rank1, n1 = _top_k_no_ties(s1)
            v2, rank2, n2 = _top_k_no_ties(s2)
            counts, picked, n12 = _merge_no_ties(v1, v2)
            emit(h, c, s1, s2, rank1, rank2, v1[0], v2[0], counts, picked)
            k = float(PEER_TOPK)
            flags.append(jnp.abs(n1 - k) + jnp.abs(n2 - k) + jnp.abs(n12 - k))

        def redo(c):
            s1 = s_scr[2 * h, c]
            s2 = s_scr[2 * h + 1, c]
            sv1, rank1 = _top_k_ranked(s1)
            sv2, rank2 = _top_k_ranked(s2)
            count, picked = _merge_top_k(sv1, sv2)
            counts = [count[a:a + 1] for a in range(PEER_TOPK)]
            emit(h, c, s1, s2, rank1, rank2, sv1[0:1], sv2[0:1], counts, picked)

        @pl.when(jnp.max(sum(flags)) > 0.0)
        def _():
            for dc in range(group):
                pl.when(jnp.max(flags[dc]) > 0.0)(functools.partial(redo, c0 + dc))

        return carry

    lax.fori_loop(0, PEER_HEADS * n_chunks // group, body, 0)


def _route(x2d, w_q, sub_keys):
    t, d = x2d.shape
    tb = min(ROW_BLOCK, t)
    nc = tb // LANES
    table_shape = (PEER_HEADS, t // LANES, PEER_N_KEYS, LANES)
    table_spec = pl.BlockSpec((PEER_HEADS, nc, PEER_N_KEYS, LANES), lambda i: (0, i, 0, 0))
    return pl.pallas_call(
        _route_kernel,
        grid=(t // tb,),
        in_specs=[
            pl.BlockSpec((tb, d), lambda i: (i, 0)),
            _full((PEER_HEADS * PEER_D_QUERY, d)),
            _full((2, PEER_N_KEYS, PEER_D_QUERY // 2)),
        ],
        out_specs=[table_spec] * 4,
        out_shape=[jax.ShapeDtypeStruct(table_shape, F32)] * 4,
        scratch_shapes=[pltpu.VMEM((2 * PEER_HEADS, nc, PEER_N_KEYS, LANES), F32)],
        compiler_params=_params(1),
        name="peer_route",
    )(x2d, w_q.T.astype(BF16), sub_keys.astype(BF16))


def _expert_kernel(x_ref, u_ref, vt_ref, rank2_ref, e2_ref, len1_ref, e1_ref, g_ref, b_ref,
                   o_ref, xb_scr, h_scr, a_scr, acc_scr, rank2_scr, e2_scr, *, alpha):
    tb = x_ref.shape[0]
    n_sub, sub = h_scr.shape[0], h_scr.shape[1]
    n_chunks = tb // LANES
    slabs_per_sub = sub // PEER_N_KEYS
    pack = 16
    j = pl.program_id(1)

    @pl.when(j == 0)
    def _():
        xb_scr[...] = x_ref[...].astype(BF16)
        acc_scr[...] = jnp.zeros(acc_scr.shape, F32)
        for h in range(PEER_HEADS):
            for c in range(n_chunks):
                rank2_scr[h, c] = rank2_ref[h, c].astype(BF16)
                e2_scr[h, c] = e2_ref[h, c].astype(BF16)

    def pre_activations(q):
        h_scr[q] = lax.dot_general(u_ref[q * sub:(q + 1) * sub, :], xb_scr[...], NT_DIMS,
                                   preferred_element_type=F32)

    def gated_activations(q):
        for s in range(slabs_per_sub):
            i1 = (j * n_sub + q) * slabs_per_sub + s
            r0 = s * PEER_N_KEYS
            for c in range(n_chunks):
                lanes = slice(c * LANES, (c + 1) * LANES)
                length, e1 = [], []
                for h in range(PEER_HEADS):
                    row = len1_ref[h, c, pl.ds(i1, 1), :]
                    length.append(jnp.broadcast_to(row, (pack, LANES)).astype(BF16))
                    row = e1_ref[h, c, pl.ds(i1, 1), :]
                    e1.append(jnp.broadcast_to(row, (pack, LANES)).astype(BF16))
                for k in range(PEER_N_KEYS // pack):
                    rows = slice(k * pack, (k + 1) * pack)
                    gate = None
                    for h in range(PEER_HEADS):
                        term = jnp.where(rank2_scr[h, c, rows, :] < length[h],
                                         e2_scr[h, c, rows, :], jnp.zeros((), BF16)) * e1[h]
                        gate = term if gate is None else gate + term
                    pre = h_scr[q, r0 + k * pack:r0 + (k + 1) * pack, lanes]
                    act = pre * (1.0 + lax.erf(pre * (2.0 ** -0.5)))
                    a_scr[q, r0 + k * pack:r0 + (k + 1) * pack, lanes] = act.astype(BF16) * gate

    def accumulate(q):
        acc_scr[...] += jnp.dot(vt_ref[:, q * sub:(q + 1) * sub], a_scr[q],
                                preferred_element_type=F32)

    pre_activations(0)
    for q in range(n_sub):
        if q + 1 < n_sub:
            pre_activations(q + 1)
        gated_activations(q)
        accumulate(q)

    @pl.when(j == pl.num_programs(1) - 1)
    def _():
        y = acc_scr[...].T
        o_ref[...] = _layer_norm(alpha * x_ref[...] + y, g_ref[...], b_ref[...])


def _experts(x2d, tables, u, v, g, b, alpha):
    t, d = x2d.shape
    n_experts = u.shape[0]
    tb = min(ROW_BLOCK, t)
    te = EXPERT_TILE
    n_sub = te // EXPERT_SUB_TILE
    nc = tb // LANES
    table_spec = pl.BlockSpec((PEER_HEADS, nc, PEER_N_KEYS, LANES), lambda i, j: (0, i, 0, 0))
    return pl.pallas_call(
        functools.partial(_expert_kernel, alpha=alpha),
        grid=(t // tb, n_experts // te),
        in_specs=[
            pl.BlockSpec((tb, d), lambda i, j: (i, 0)),
            pl.BlockSpec((te, d), lambda i, j: (j, 0)),
            pl.BlockSpec((d, te), lambda i, j: (0, j)),
            table_spec, table_spec, table_spec, table_spec,
            pl.BlockSpec((1, d), lambda i, j: (0, 0)),
            pl.BlockSpec((1, d), lambda i, j: (0, 0)),
        ],
        out_specs=pl.BlockSpec((tb, d), lambda i, j: (i, 0)),
        out_shape=jax.ShapeDtypeStruct((t, d), F32),
        scratch_shapes=[
            pltpu.VMEM((tb, d), BF16),
            pltpu.VMEM((n_sub, EXPERT_SUB_TILE, tb), F32),
            pltpu.VMEM((n_sub, EXPERT_SUB_TILE, tb), BF16),
            pltpu.VMEM((d, tb), F32),
            pltpu.VMEM((PEER_HEADS, nc, PEER_N_KEYS, LANES), BF16),
            pltpu.VMEM((PEER_HEADS, nc, PEER_N_KEYS, LANES), BF16),
        ],
        compiler_params=_params(2),
        name="peer_experts",
    )(x2d, u.astype(BF16), v.T.astype(BF16), *tables, g.reshape(1, -1), b.reshape(1, -1))


def _peer_layer(x2d, w_q, sub_keys, u, v, g, b, alpha):
    tables = _route(x2d, w_q, sub_keys)
    return _experts(x2d, tables, u, v, g, b, alpha)


def kernel(x, positions, conv_w_in, conv_b_in, conv_w_dw, conv_b_dw, conv_ln_g, conv_ln_b,
           conv_w_out, conv_b_out, mla_w_in, mla_q_norm, mla_kv_norm, mla_w_uq, mla_w_uk,
           mla_w_uv, mla_w_out, peer_w_q, peer_sub_keys, peer_u, peer_v,
           ln1_g, ln1_b, ln2_g, ln2_b):
    batch, seq, d = x.shape
    depth = ln1_g.shape[0]
    alpha = (2 * depth) ** 0.25
    x2d = x.reshape(batch * seq, d)

    inv = 1.0 / (ROPE_THETA ** (jnp.arange(0, MLA_ROPE, 2, dtype=F32) / MLA_ROPE))
    ang = positions.astype(F32).reshape(batch * seq, 1) * inv
    ang = _pad_lanes(jnp.concatenate([ang, ang], axis=-1))

    for i in range(depth):
        j = i // 2
        if i % 2 == 0:
            x2d = _conv_layer(x2d, seq, conv_w_in[j], conv_b_in[j], conv_w_dw[j], conv_b_dw[j],
                              conv_ln_g[j], conv_ln_b[j], conv_w_out[j], conv_b_out[j],
                              ln1_g[i], ln1_b[i], alpha)
        else:
            x2d = _mla_layer(x2d, ang, batch, seq, mla_w_in[j], mla_q_norm[j], mla_kv_norm[j],
                             mla_w_uq[j], mla_w_uk[j], mla_w_uv[j], mla_w_out[j],
                             ln1_g[i], ln1_b[i], alpha)
        x2d = _peer_layer(x2d, peer_w_q[i], peer_sub_keys[i], peer_u[i], peer_v[i],
                          ln2_g[i], ln2_b[i], alpha)
    return x2d.reshape(batch, seq, d)
```

```python
import functools

import jax
import jax.numpy as jnp
from jax import lax
from jax.experimental import pallas as pl
from jax.experimental.pallas import tpu as pltpu

F32 = jnp.float32
BF16 = jnp.bfloat16

LANES = 128
VMEM_LIMIT = 58 * 1024 * 1024

D_MODEL = 1024
CHUNK = 64
CONV_WIDTH = 31
CONV_HALO = 32

MLA_HEADS = 8
MLA_NOPE = 128
MLA_ROPE = 64
MLA_V = 128
MLA_Q_RANK = 384
MLA_KV_RANK = 256
ROPE_THETA = 10000.0
QK_PAD = 256

PEER_HEADS = 8
PEER_N_KEYS = 128
PEER_D_QUERY = 256
PEER_TOPK = 16

LN_EPS = 1e-5
RMS_EPS = 1e-6

ROW_BLOCK = 512
EXPERT_TILE = 2048

NT_DIMS = (((1,), (1,)), ((), ()))


def _params(n_axes, flags=None):
    return pltpu.CompilerParams(
        dimension_semantics=("arbitrary",) * n_axes, vmem_limit_bytes=VMEM_LIMIT, flags=flags)


def _layer_norm(x, g, b):
    mu = jnp.mean(x, axis=-1, keepdims=True)
    xc = x - mu
    var = jnp.mean(xc * xc, axis=-1, keepdims=True)
    return xc * lax.rsqrt(var + LN_EPS) * g + b


def _rms_norm(x, g):
    return x * lax.rsqrt(jnp.mean(x * x, axis=-1, keepdims=True) + RMS_EPS) * g


def _full(shape):
    return pl.BlockSpec(shape, lambda *_: (0,) * len(shape))


def _conv_layer_kernel(x_ref, w_in_ref, b_in_ref, w_dw_ref, b_dw_ref, lng_ref, lnb_ref,
                       w_out_ref, b_out_ref, l1g_ref, l1b_ref, o_ref, gbuf, cbuf, *,
                       blocks_per_seq, alpha):
    tm, d = x_ref.shape
    i = pl.program_id(0)

    @pl.when(i % blocks_per_seq == 0)
    def _():
        gbuf[0:CONV_HALO, :] = jnp.zeros((CONV_HALO, d), F32)

    x = x_ref[...]
    h = jnp.dot(x.astype(BF16), w_in_ref[...], preferred_element_type=F32) + b_in_ref[...]
    gbuf[CONV_HALO:CONV_HALO + tm, :] = h[:, :d] * jax.nn.sigmoid(h[:, d:])

    base = CONV_HALO - (CONV_WIDTH - 1)
    sublanes = 8
    taps_by_shift = {}
    for k in range(CONV_WIDTH):
        taps_by_shift.setdefault((base + k) % sublanes, []).append((k, (base + k) // sublanes))
    rc, lc = min(128, tm), 256
    for r0 in range(0, tm, rc):
        for l0 in range(0, d, lc):
            lanes = slice(l0, l0 + lc)
            acc = jnp.broadcast_to(b_dw_ref[:, lanes], (rc, lc))
            for shift, taps in taps_by_shift.items():
                span = sublanes * max(m for _, m in taps) + rc
                if shift:
                    n = span + sublanes
                    win = pltpu.roll(gbuf[r0:r0 + n, lanes], n - shift, axis=0)
                else:
                    win = gbuf[r0:r0 + span, lanes]
                for k, m in taps:
                    acc = acc + w_dw_ref[k:k + 1, lanes] * win[sublanes * m:sublanes * m + rc]
            cbuf[r0:r0 + rc, lanes] = acc
    gbuf[0:CONV_HALO, :] = gbuf[tm:tm + CONV_HALO, :]

    y = _layer_norm(cbuf[...], lng_ref[...], lnb_ref[...])
    y = y * jax.nn.sigmoid(y)
    mix = jnp.dot(y.astype(BF16), w_out_ref[...], preferred_element_type=F32) + b_out_ref[...]
    o_ref[...] = _layer_norm(alpha * x + mix, l1g_ref[...], l1b_ref[...])


def _conv_layer(x2d, seq, w_in, b_in, w_dw, b_dw, ln_g, ln_b, w_out, b_out, l1g, l1b, alpha):
    t, d = x2d.shape
    tm = min(ROW_BLOCK, seq)
    row = lambda v: v.reshape(1, -1)
    return pl.pallas_call(
        functools.partial(_conv_layer_kernel, blocks_per_seq=seq // tm, alpha=alpha),
        grid=(t // tm,),
        in_specs=[
            pl.BlockSpec((tm, d), lambda i: (i, 0)),
            _full((d, 2 * d)), _full((1, 2 * d)), _full((CONV_WIDTH, d)), _full((1, d)),
            _full((1, d)), _full((1, d)), _full((d, d)), _full((1, d)), _full((1, d)),
            _full((1, d)),
        ],
        out_specs=pl.BlockSpec((tm, d), lambda i: (i, 0)),
        out_shape=jax.ShapeDtypeStruct((t, d), F32),
        scratch_shapes=[pltpu.VMEM((CONV_HALO + tm, d), F32), pltpu.VMEM((tm, d), F32)],
        compiler_params=_params(1),
        name="conv_layer",
    )(x2d, w_in.astype(BF16), row(b_in), w_dw, row(b_dw), row(ln_g), row(ln_b),
      w_out.astype(BF16), row(b_out), row(l1g), row(l1b))


def _mla_proj_kernel(x_ref, ang_ref, w_in_ref, qn_ref, kvn_ref, w_uq_ref, w_uk_ref, w_uv_ref,
                     q_ref, k_ref, v_ref, *, scale):
    x = x_ref[...].astype(BF16)
    c = jnp.dot(x, w_in_ref[...], preferred_element_type=F32)
    cq = _rms_norm(c[:, :MLA_Q_RANK], qn_ref[...])
    kv_end = MLA_Q_RANK + MLA_KV_RANK
    ckv = _rms_norm(c[:, MLA_Q_RANK:kv_end], kvn_ref[...])
    ang = ang_ref[...]
    cos, sin = jnp.cos(ang), jnp.sin(ang)
    k_rope = c[:, kv_end:kv_end + LANES] * cos + c[:, kv_end + LANES:kv_end + 2 * LANES] * sin

    q = jnp.dot(cq.astype(BF16), w_uq_ref[...], preferred_element_type=F32)
    ckv_b = ckv.astype(BF16)
    kn = jnp.dot(ckv_b, w_uk_ref[...], preferred_element_type=F32)
    v_ref[...] = jnp.dot(ckv_b, w_uv_ref[...], preferred_element_type=F32).astype(BF16)
    k_rope_b = k_rope.astype(BF16)
    for h in range(MLA_HEADS):
        qh = q[:, h * 3 * LANES:(h + 1) * 3 * LANES]
        q_rope = qh[:, LANES:2 * LANES] * cos + qh[:, 2 * LANES:] * sin
        q_ref[:, h * QK_PAD:h * QK_PAD + LANES] = (qh[:, :LANES] * scale).astype(BF16)
        q_ref[:, h * QK_PAD + LANES:(h + 1) * QK_PAD] = (q_rope * scale).astype(BF16)
        k_ref[:, h * QK_PAD:h * QK_PAD + LANES] = kn[:, h * MLA_NOPE:(h + 1) * MLA_NOPE].astype(BF16)
        k_ref[:, h * QK_PAD + LANES:(h + 1) * QK_PAD] = k_rope_b


def _half_rotation(w):
    half = w.shape[-1] // 2
    return jnp.concatenate([-w[..., half:], w[..., :half]], axis=-1)


def _pad_lanes(w):
    return jnp.pad(w, [(0, 0)] * (w.ndim - 1) + [(0, LANES - w.shape[-1])])


def _mla_proj(x2d, ang, w_in, q_norm, kv_norm, w_uq, w_uk, w_uv):
    t, d = x2d.shape
    tm = min(ROW_BLOCK, t)
    kv_end = MLA_Q_RANK + MLA_KV_RANK
    w_rope = w_in[:, kv_end:]
    w_in_ext = jnp.concatenate(
        [w_in[:, :kv_end], _pad_lanes(w_rope), _pad_lanes(_half_rotation(w_rope))], axis=1).astype(BF16)
    wq = w_uq.reshape(MLA_Q_RANK, MLA_HEADS, MLA_NOPE + MLA_ROPE)
    wq_rope = wq[..., MLA_NOPE:]
    w_uq_ext = jnp.concatenate(
        [wq[..., :MLA_NOPE], _pad_lanes(wq_rope), _pad_lanes(_half_rotation(wq_rope))],
        axis=-1).reshape(MLA_Q_RANK, MLA_HEADS * 3 * LANES).astype(BF16)
    scale = (MLA_NOPE + MLA_ROPE) ** -0.5
    n_in = w_in_ext.shape[1]
    return pl.pallas_call(
        functools.partial(_mla_proj_kernel, scale=scale),
        grid=(t // tm,),
        in_specs=[
            pl.BlockSpec((tm, d), lambda i: (i, 0)),
            pl.BlockSpec((tm, LANES), lambda i: (i, 0)),
            _full((d, n_in)), _full((1, MLA_Q_RANK)), _full((1, MLA_KV_RANK)),
            _full((MLA_Q_RANK, MLA_HEADS * 3 * LANES)),
            _full((MLA_KV_RANK, MLA_HEADS * MLA_NOPE)), _full((MLA_KV_RANK, MLA_HEADS * MLA_V)),
        ],
        out_specs=[
            pl.BlockSpec((tm, MLA_HEADS * QK_PAD), lambda i: (i, 0)),
            pl.BlockSpec((tm, MLA_HEADS * QK_PAD), lambda i: (i, 0)),
            pl.BlockSpec((tm, MLA_HEADS * MLA_V), lambda i: (i, 0)),
        ],
        out_shape=[
            jax.ShapeDtypeStruct((t, MLA_HEADS * QK_PAD), BF16),
            jax.ShapeDtypeStruct((t, MLA_HEADS * QK_PAD), BF16),
            jax.ShapeDtypeStruct((t, MLA_HEADS * MLA_V), BF16),
        ],
        compiler_params=_params(1),
        name="mla_proj",
    )(x2d, ang, w_in_ext, q_norm.reshape(1, -1), kv_norm.reshape(1, -1), w_uq_ext,
      w_uk.astype(BF16), w_uv.astype(BF16))


def _attn_kernel(q_ref, k_ref, v_ref, o_ref):
    tq = q_ref.shape[0]
    qi = pl.program_id(2)
    q = q_ref[...]

    def step(k, v, carry, mask):
        m, l, acc = carry
        s = lax.dot_general(q, k, NT_DIMS, preferred_element_type=F32)
        if mask is not None:
            s = jnp.where(mask, s, jnp.finfo(F32).min)
        m_new = jnp.maximum(m, jnp.max(s, axis=-1, keepdims=True))
        corr = jnp.exp(m - m_new)
        p = jnp.exp(s - m_new)
        l = corr * l + jnp.sum(p, axis=-1, keepdims=True)
        acc = corr * acc + jnp.dot(p.astype(BF16), v, preferred_element_type=F32)
        return m_new, l, acc

    def body(kb, carry):
        r0 = pl.multiple_of(kb * tq, tq)
        return step(k_ref[pl.ds(r0, tq), :], v_ref[pl.ds(r0, tq), :], carry, None)

    init = (jnp.full((tq, 1), -jnp.inf, F32), jnp.zeros((tq, 1), F32), jnp.zeros((tq, MLA_V), F32))
    carry = lax.fori_loop(0, qi, body, init)
    chunk_bits = CHUNK.bit_length() - 1
    q_chunk = lax.shift_right_logical(lax.broadcasted_iota(jnp.int32, (tq, tq), 0), chunk_bits)
    k_chunk = lax.shift_right_logical(lax.broadcasted_iota(jnp.int32, (tq, tq), 1), chunk_bits)
    r0 = pl.multiple_of(qi * tq, tq)
    _, l, acc = step(k_ref[pl.ds(r0, tq), :], v_ref[pl.ds(r0, tq), :], carry, k_chunk <= q_chunk)
    o_ref[...] = (acc / l).astype(BF16)


def _attention(q, k, v, batch, seq):
    t = q.shape[0]
    tq = min(ROW_BLOCK, seq)
    nq = seq // tq
    return pl.pallas_call(
        _attn_kernel,
        grid=(batch, MLA_HEADS, nq),
        in_specs=[
            pl.BlockSpec((tq, QK_PAD), lambda b, h, i: (b * nq + i, h)),
            pl.BlockSpec((seq, QK_PAD), lambda b, h, i: (b, h)),
            pl.BlockSpec((seq, MLA_V), lambda b, h, i: (b, h)),
        ],
        out_specs=pl.BlockSpec((tq, MLA_V), lambda b, h, i: (b * nq + i, h)),
        out_shape=jax.ShapeDtypeStruct((t, MLA_HEADS * MLA_V), BF16),
        compiler_params=_params(3),
        name="mla_attention",
    )(q, k, v)


def _out_proj_kernel(o_ref, x_ref, w_ref, g_ref, b_ref, y_ref, *, alpha):
    mix = jnp.dot(o_ref[...], w_ref[...], preferred_element_type=F32)
    y_ref[...] = _layer_norm(alpha * x_ref[...] + mix, g_ref[...], b_ref[...])


def _out_proj(o, x2d, w_out, g, b, alpha):
    t, d = x2d.shape
    tm = min(ROW_BLOCK, t)
    return pl.pallas_call(
        functools.partial(_out_proj_kernel, alpha=alpha),
        grid=(t // tm,),
        in_specs=[
            pl.BlockSpec((tm, o.shape[1]), lambda i: (i, 0)),
            pl.BlockSpec((tm, d), lambda i: (i, 0)),
            _full(w_out.shape), _full((1, d)), _full((1, d)),
        ],
        out_specs=pl.BlockSpec((tm, d), lambda i: (i, 0)),
        out_shape=jax.ShapeDtypeStruct((t, d), F32),
        compiler_params=_params(1),
        name="mla_out_proj",
    )(o, x2d, w_out.astype(BF16), g.reshape(1, -1), b.reshape(1, -1))


def _mla_layer(x2d, ang, batch, seq, w_in, q_norm, kv_norm, w_uq, w_uk, w_uv, w_out, l1g, l1b, alpha):
    q, k, v = _mla_proj(x2d, ang, w_in, q_norm, kv_norm, w_uq, w_uk, w_uv)
    o = _attention(q, k, v, batch, seq)
    return _out_proj(o, x2d, w_out, l1g, l1b, alpha)


def _top_k_ranked(s):
    n = s.shape[0]
    key_id = lax.broadcasted_iota(jnp.int32, s.shape, 0)
    rank = jnp.full(s.shape, float(PEER_TOPK), F32)
    vals = []
    for r in range(PEER_TOPK):
        m = jnp.max(s, axis=0, keepdims=True)
        first = jnp.min(jnp.where(s == m, key_id, n), axis=0, keepdims=True)
        hit = key_id == first
        rank = jnp.where(hit, float(r), rank)
        s = jnp.where(hit, -jnp.inf, s)
        vals.append(m)
    return jnp.concatenate(vals, axis=0), rank


def _merge_top_k(sv1, sv2):
    row_id = lax.broadcasted_iota(jnp.int32, sv1.shape, 0).astype(F32)
    count = jnp.zeros(sv1.shape, F32)
    front = sv1 + sv2[0:1]
    picked = []
    for r in range(PEER_TOPK):
        m = jnp.max(front, axis=0, keepdims=True)
        picked.append(m)
        first = jnp.min(jnp.where(front == m, row_id, float(PEER_TOPK)), axis=0, keepdims=True)
        hit = row_id == first
        count = jnp.where(hit, count + 1.0, count)
        if r + 1 < PEER_TOPK:
            new_len = jnp.sum(jnp.where(hit, count, 0.0), axis=0, keepdims=True)
            a_val = jnp.sum(jnp.where(hit, sv1, 0.0), axis=0, keepdims=True)
            b_val = jnp.sum(jnp.where(row_id == new_len, sv2, 0.0), axis=0, keepdims=True)
            nxt = jnp.where(new_len >= float(PEER_TOPK), -jnp.inf, a_val + b_val)
            front = jnp.where(hit, nxt, front)
    return count, picked


def _top_k_no_ties(s):
    rank = jnp.full(s.shape, float(PEER_TOPK), F32)
    vals = []
    for r in range(PEER_TOPK):
        m = jnp.max(s, axis=0, keepdims=True)
        hit = s == m
        rank = jnp.where(hit, float(r), rank)
        s = jnp.where(hit, -jnp.inf, s)
        vals.append(m)
    members = jnp.sum(jnp.where(rank < float(PEER_TOPK), 1.0, 0.0), axis=0, keepdims=True)
    return vals, rank, members


def _merge_no_ties(v1, v2):
    sub = 8
    lo2 = jnp.concatenate(v2[:sub], axis=0)
    hi2 = jnp.concatenate(v2[sub:], axis=0)
    hi1 = jnp.concatenate(v1[sub:], axis=0)
    row = lax.broadcasted_iota(jnp.int32, lo2.shape, 0)
    cand, valid = [], []
    for a in range(sub):
        lim = min(PEER_TOPK // (a + 1), sub)
        ok = row < lim
        cand.append(jnp.where(ok, v1[a] + lo2, -jnp.inf))
        valid.append(ok)
    cand.append(v1[0] + hi2)
    cand.append(hi1 + v2[0])
    valid += [None, None]
    picked = []
    for r in range(PEER_TOPK):
        m = cand[0]
        for c in cand[1:]:
            m = jnp.maximum(m, c)
        m = jnp.max(m, axis=0, keepdims=True)
        picked.append(m)
        cand = [jnp.where(c == m, -jnp.inf, c) for c in cand]
    taken = []
    for c, ok in zip(cand, valid):
        t = c == -jnp.inf
        taken.append(jnp.where(t if ok is None else t & ok, 1.0, 0.0))
    counts = [jnp.sum(t, axis=0, keepdims=True) for t in taken[:sub]]
    counts[0] = counts[0] + jnp.sum(taken[sub], axis=0, keepdims=True)
    counts += [taken[sub + 1][a:a + 1] for a in range(sub)]
    total = counts[0]
    for cnt in counts[1:]:
        total = total + cnt
    return counts, picked, total


def _route_kernel(x_ref, wq_ref, keys_ref, rank2_ref, e2_ref, len1_ref, e1_ref, s_scr):
    tb = x_ref.shape[0]
    n_chunks = tb // LANES
    half = PEER_D_QUERY // 2
    xb = x_ref[...].astype(BF16)
    q_t = lax.dot_general(wq_ref[...], xb, NT_DIMS, preferred_element_type=F32)
    for h in range(PEER_HEADS):
        for p in range(2):
            r0 = h * PEER_D_QUERY + p * half
            s = jnp.dot(keys_ref[p], q_t[r0:r0 + half, :].astype(BF16), preferred_element_type=F32)
            for c in range(n_chunks):
                s_scr[2 * h + p, c] = s[:, c * LANES:(c + 1) * LANES]

    def emit(h, c, s1, s2, rank1, rank2, top1, top2, counts, picked):
        z = jnp.ones_like(picked[0])
        for r in range(1, PEER_TOPK):
            z = z + jnp.exp(picked[r] - picked[0])
        len1 = jnp.zeros(s1.shape, F32)
        for a in range(PEER_TOPK):
            len1 = jnp.where(rank1 == float(a), counts[a], len1)
        rank2_ref[h, c] = rank2
        e2_ref[h, c] = jnp.exp(s2 - top2)
        len1_ref[h, c] = len1
        e1_ref[h, c] = jnp.exp(s1 - top1) * (0.5 / z)

    group = min(4, n_chunks)

    def body(idx, carry):
        h = idx // (n_chunks // group)
        c0 = (idx % (n_chunks // group)) * group
        flags = []
        for dc in range(group):
            c = c0 + dc
            s1 = s_scr[2 * h, c]
            s2 = s_scr[2 * h + 1, c]
            v1, rank1, n1 = _top_k_no_ties(s1)
            v2, rank2, n2 = _top_k_no_ties(s2)
            counts, picked, n12 = _merge_no_ties(v1, v2)
            emit(h, c, s1, s2, rank1, rank2, v1[0], v2[0], counts, picked)
            k = float(PEER_TOPK)
            flags.append(jnp.abs(n1 - k) + jnp.abs(n2 - k) + jnp.abs(n12 - k))

        def redo(c):
            s1 = s_scr[2 * h, c]
            s2 = s_scr[2 * h + 1, c]
            sv1, rank1 = _top_k_ranked(s1)
            sv2, rank2 = _top_k_ranked(s2)
            count, picked = _merge_top_k(sv1, sv2)
            counts = [count[a:a + 1] for a in range(PEER_TOPK)]
            emit(h, c, s1, s2, rank1, rank2, sv1[0:1], sv2[0:1], counts, picked)

        @pl.when(jnp.max(sum(flags)) > 0.0)
        def _():
            for dc in range(group):
                pl.when(jnp.max(flags[dc]) > 0.0)(functools.partial(redo, c0 + dc))

        return carry

    lax.fori_loop(0, PEER_HEADS * n_chunks // group, body, 0)


def _route(x2d, w_q, sub_keys):
    t, d = x2d.shape
    tb = min(ROW_BLOCK, t)
    nc = tb // LANES
    table_shape = (PEER_HEADS, t // LANES, PEER_N_KEYS, LANES)
    table_spec = pl.BlockSpec((PEER_HEADS, nc, PEER_N_KEYS, LANES), lambda i: (0, i, 0, 0))
    return pl.pallas_call(
        _route_kernel,
        grid=(t // tb,),
        in_specs=[
            pl.BlockSpec((tb, d), lambda i: (i, 0)),
            _full((PEER_HEADS * PEER_D_QUERY, d)),
            _full((2, PEER_N_KEYS, PEER_D_QUERY // 2)),
        ],
        out_specs=[table_spec] * 4,
        out_shape=[jax.ShapeDtypeStruct(table_shape, F32)] * 4,
        scratch_shapes=[pltpu.VMEM((2 * PEER_HEADS, nc, PEER_N_KEYS, LANES), F32)],
        compiler_params=_params(1),
        name="peer_route",
    )(x2d, w_q.T.astype(BF16), sub_keys.astype(BF16))


def _expert_kernel(x_ref, u_ref, vt_ref, rank2_ref, e2_ref, len1_ref, e1_ref, g_ref, b_ref,
                   o_ref, xt_scr, h_scr, a_scr, acc_scr, rank2_scr, e2_scr, *, alpha, n_tiles):
    tb = x_ref.shape[0]
    te = u_ref.shape[0]
    slabs = te // PEER_N_KEYS
    n_chunks = tb // LANES
    pack = 16
    j = pl.program_id(1)

    @pl.when(j == 0)
    def _():
        xt_scr[...] = x_ref[...].T.astype(BF16)
        acc_scr[...] = jnp.zeros(acc_scr.shape, F32)
        for h in range(PEER_HEADS):
            for c in range(n_chunks):
                rank2_scr[h, c] = rank2_ref[h, c].astype(BF16)
                e2_scr[h, c] = e2_ref[h, c].astype(BF16)

    h_scr[...] = jnp.dot(u_ref[...], xt_scr[...], preferred_element_type=F32)

    def slab(s):
        i1 = j * slabs + s
        r0 = s * PEER_N_KEYS
        for c in range(n_chunks):
            lanes = slice(c * LANES, (c + 1) * LANES)
            length, e1 = [], []
            for h in range(PEER_HEADS):
                row = len1_ref[h, c, pl.ds(i1, 1), :]
                length.append(jnp.broadcast_to(row, (pack, LANES)).astype(BF16))
                row = e1_ref[h, c, pl.ds(i1, 1), :]
                e1.append(jnp.broadcast_to(row, (pack, LANES)).astype(BF16))
            for k in range(PEER_N_KEYS // pack):
                rows = slice(k * pack, (k + 1) * pack)
                gate = None
                for h in range(PEER_HEADS):
                    term = jnp.where(rank2_scr[h, c, rows, :] < length[h],
                                     e2_scr[h, c, rows, :], jnp.zeros((), BF16)) * e1[h]
                    gate = term if gate is None else gate + term
                pre = h_scr[r0 + k * pack:r0 + (k + 1) * pack, lanes]
                act = pre * (1.0 + lax.erf(pre * (2.0 ** -0.5)))
                a_scr[r0 + k * pack:r0 + (k + 1) * pack, lanes] = act.astype(BF16) * gate

    @pl.when(j >= 0)
    def _():
        for s in range(slabs):
            slab(s)

    acc_scr[...] += jnp.dot(vt_ref[...], a_scr[...], preferred_element_type=F32)

    @pl.when(j == n_tiles - 1)
    def _():
        y = acc_scr[...].T
        o_ref[...] = _layer_norm(alpha * x_ref[...] + y, g_ref[...], b_ref[...])


def _experts(x2d, tables, u, v, g, b, alpha):
    t, d = x2d.shape
    n_experts = u.shape[0]
    tb = min(ROW_BLOCK, t)
    te = EXPERT_TILE
    n_tiles = n_experts // te
    nc = tb // LANES
    table_spec = pl.BlockSpec((PEER_HEADS, nc, PEER_N_KEYS, LANES), lambda i, j: (0, i, 0, 0))
    return pl.pallas_call(
        functools.partial(_expert_kernel, alpha=alpha, n_tiles=n_tiles),
        grid=(t // tb, n_tiles),
        in_specs=[
            pl.BlockSpec((tb, d), lambda i, j: (i, 0)),
            pl.BlockSpec((te, d), lambda i, j: (j, 0)),
            pl.BlockSpec((d, te), lambda i, j: (0, j)),
            table_spec, table_spec, table_spec, table_spec,
            pl.BlockSpec((1, d), lambda i, j: (0, 0)),
            pl.BlockSpec((1, d), lambda i, j: (0, 0)),
        ],
        out_specs=pl.BlockSpec((tb, d), lambda i, j: (i, 0)),
        out_shape=jax.ShapeDtypeStruct((t, d), F32),
        scratch_shapes=[
            pltpu.VMEM((d, tb), BF16),
            pltpu.VMEM((te, tb), F32),
            pltpu.VMEM((te, tb), BF16),
            pltpu.VMEM((d, tb), F32),
            pltpu.VMEM((PEER_HEADS, nc, PEER_N_KEYS, LANES), BF16),
            pltpu.VMEM((PEER_HEADS, nc, PEER_N_KEYS, LANES), BF16),
        ],
        compiler_params=_params(2),
        name="peer_experts",
    )(x2d, u.astype(BF16), v.T.astype(BF16), *tables, g.reshape(1, -1), b.reshape(1, -1))


def _peer_layer(x2d, w_q, sub_keys, u, v, g, b, alpha):
    tables = _route(x2d, w_q, sub_keys)
    return _experts(x2d, tables, u, v, g, b, alpha)


def kernel(x, positions, conv_w_in, conv_b_in, conv_w_dw, conv_b_dw, conv_ln_g, conv_ln_b,
           conv_w_out, conv_b_out, mla_w_in, mla_q_norm, mla_kv_norm, mla_w_uq, mla_w_uk,
           mla_w_uv, mla_w_out, peer_w_q, peer_sub_keys, peer_u, peer_v,
           ln1_g, ln1_b, ln2_g, ln2_b):
    batch, seq, d = x.shape
    depth = ln1_g.shape[0]
    alpha = (2 * depth) ** 0.25
    x2d = x.reshape(batch * seq, d)

    inv = 1.0 / (ROPE_THETA ** (jnp.arange(0, MLA_ROPE, 2, dtype=F32) / MLA_ROPE))
    ang = positions.astype(F32).reshape(batch * seq, 1) * inv
    ang = _pad_lanes(jnp.concatenate([ang, ang], axis=-1))

    for i in range(depth):
        j = i // 2
        if i % 2 == 0:
            x2d = _conv_layer(x2d, seq, conv_w_in[j], conv_b_in[j], conv_w_dw[j], conv_b_dw[j],
                              conv_ln_g[j], conv_ln_b[j], conv_w_out[j], conv_b_out[j],
                              ln1_g[i], ln1_b[i], alpha)
        else:
            x2d = _mla_layer(x2d, ang, batch, seq, mla_w_in[j], mla_q_norm[j], mla_kv_norm[j],
                             mla_w_uq[j], mla_w_uk[j], mla_w_uv[j], mla_w_out[j],
                             ln1_g[i], ln1_b[i], alpha)
        x2d = _peer_layer(x2d, peer_w_q[i], peer_sub_keys[i], peer_u[i], peer_v[i],
                          ln2_g[i], ln2_b[i], alpha)
    return x2d.reshape(batch, seq, d)
```

```python
import functools

import jax
import jax.numpy as jnp
from jax import lax
from jax.experimental import pallas as pl
from jax.experimental.pallas import tpu as pltpu

F32 = jnp.float32
BF16 = jnp.bfloat16
F8 = jnp.float8_e4m3fn

LANES = 128
VMEM_LIMIT = 58 * 1024 * 1024

D_MODEL = 1024
CHUNK = 64
CONV_WIDTH = 31
CONV_HALO = 32

MLA_HEADS = 8
MLA_NOPE = 128
MLA_ROPE = 64
MLA_V = 128
MLA_Q_RANK = 384
MLA_KV_RANK = 256
ROPE_THETA = 10000.0
QK_PAD = 256

PEER_HEADS = 8
PEER_N_KEYS = 128
PEER_D_QUERY = 256
PEER_TOPK = 16

LN_EPS = 1e-5
RMS_EPS = 1e-6

ROW_BLOCK = 512
EXPERT_TILE = 2048
FP8_TARGET = 224.0

NT_DIMS = (((1,), (1,)), ((), ()))


def _params(n_axes, flags=None):
    return pltpu.CompilerParams(
        dimension_semantics=("arbitrary",) * n_axes, vmem_limit_bytes=VMEM_LIMIT, flags=flags)


def _layer_norm(x, g, b):
    mu = jnp.mean(x, axis=-1, keepdims=True)
    xc = x - mu
    var = jnp.mean(xc * xc, axis=-1, keepdims=True)
    return xc * lax.rsqrt(var + LN_EPS) * g + b


def _rms_norm(x, g):
    return x * lax.rsqrt(jnp.mean(x * x, axis=-1, keepdims=True) + RMS_EPS) * g


def _full(shape):
    return pl.BlockSpec(shape, lambda *_: (0,) * len(shape))


def _conv_layer_kernel(x_ref, w_in_ref, b_in_ref, w_dw_ref, b_dw_ref, lng_ref, lnb_ref,
                       w_out_ref, b_out_ref, l1g_ref, l1b_ref, o_ref, gbuf, cbuf, *,
                       blocks_per_seq, alpha):
    tm, d = x_ref.shape
    i = pl.program_id(0)

    @pl.when(i % blocks_per_seq == 0)
    def _():
        gbuf[0:CONV_HALO, :] = jnp.zeros((CONV_HALO, d), F32)

    x = x_ref[...]
    h = jnp.dot(x.astype(BF16), w_in_ref[...], preferred_element_type=F32) + b_in_ref[...]
    gbuf[CONV_HALO:CONV_HALO + tm, :] = h[:, :d] * jax.nn.sigmoid(h[:, d:])

    base = CONV_HALO - (CONV_WIDTH - 1)
    sublanes = 8
    taps_by_shift = {}
    for k in range(CONV_WIDTH):
        taps_by_shift.setdefault((base + k) % sublanes, []).append((k, (base + k) // sublanes))
    rc, lc = min(128, tm), 256
    for r0 in range(0, tm, rc):
        for l0 in range(0, d, lc):
            lanes = slice(l0, l0 + lc)
            acc = jnp.broadcast_to(b_dw_ref[:, lanes], (rc, lc))
            for shift, taps in taps_by_shift.items():
                span = sublanes * max(m for _, m in taps) + rc
                if shift:
                    n = span + sublanes
                    win = pltpu.roll(gbuf[r0:r0 + n, lanes], n - shift, axis=0)
                else:
                    win = gbuf[r0:r0 + span, lanes]
                for k, m in taps:
                    acc = acc + w_dw_ref[k:k + 1, lanes] * win[sublanes * m:sublanes * m + rc]
            cbuf[r0:r0 + rc, lanes] = acc
    gbuf[0:CONV_HALO, :] = gbuf[tm:tm + CONV_HALO, :]

    y = _layer_norm(cbuf[...], lng_ref[...], lnb_ref[...])
    y = y * jax.nn.sigmoid(y)
    mix = jnp.dot(y.astype(BF16), w_out_ref[...], preferred_element_type=F32) + b_out_ref[...]
    o_ref[...] = _layer_norm(alpha * x + mix, l1g_ref[...], l1b_ref[...])


def _conv_layer(x2d, seq, w_in, b_in, w_dw, b_dw, ln_g, ln_b, w_out, b_out, l1g, l1b, alpha):
    t, d = x2d.shape
    tm = min(ROW_BLOCK, seq)
    row = lambda v: v.reshape(1, -1)
    return pl.pallas_call(
        functools.partial(_conv_layer_kernel, blocks_per_seq=seq // tm, alpha=alpha),
        grid=(t // tm,),
        in_specs=[
            pl.BlockSpec((tm, d), lambda i: (i, 0)),
            _full((d, 2 * d)), _full((1, 2 * d)), _full((CONV_WIDTH, d)), _full((1, d)),
            _full((1, d)), _full((1, d)), _full((d, d)), _full((1, d)), _full((1, d)),
            _full((1, d)),
        ],
        out_specs=pl.BlockSpec((tm, d), lambda i: (i, 0)),
        out_shape=jax.ShapeDtypeStruct((t, d), F32),
        scratch_shapes=[pltpu.VMEM((CONV_HALO + tm, d), F32), pltpu.VMEM((tm, d), F32)],
        compiler_params=_params(1),
        name="conv_layer",
    )(x2d, w_in.astype(BF16), row(b_in), w_dw, row(b_dw), row(ln_g), row(ln_b),
      w_out.astype(BF16), row(b_out), row(l1g), row(l1b))


def _mla_proj_kernel(x_ref, ang_ref, w_in_ref, qn_ref, kvn_ref, w_uq_ref, w_uk_ref, w_uv_ref,
                     q_ref, k_ref, v_ref, *, scale):
    x = x_ref[...].astype(BF16)
    c = jnp.dot(x, w_in_ref[...], preferred_element_type=F32)
    cq = _rms_norm(c[:, :MLA_Q_RANK], qn_ref[...])
    kv_end = MLA_Q_RANK + MLA_KV_RANK
    ckv = _rms_norm(c[:, MLA_Q_RANK:kv_end], kvn_ref[...])
    ang = ang_ref[...]
    cos, sin = jnp.cos(ang), jnp.sin(ang)
    k_rope = c[:, kv_end:kv_end + LANES] * cos + c[:, kv_end + LANES:kv_end + 2 * LANES] * sin

    q = jnp.dot(cq.astype(BF16), w_uq_ref[...], preferred_element_type=F32)
    ckv_b = ckv.astype(BF16)
    kn = jnp.dot(ckv_b, w_uk_ref[...], preferred_element_type=F32)
    v_ref[...] = jnp.dot(ckv_b, w_uv_ref[...], preferred_element_type=F32).astype(BF16)
    k_rope_b = k_rope.astype(BF16)
    for h in range(MLA_HEADS):
        qh = q[:, h * 3 * LANES:(h + 1) * 3 * LANES]
        q_rope = qh[:, LANES:2 * LANES] * cos + qh[:, 2 * LANES:] * sin
        q_ref[:, h * QK_PAD:h * QK_PAD + LANES] = (qh[:, :LANES] * scale).astype(BF16)
        q_ref[:, h * QK_PAD + LANES:(h + 1) * QK_PAD] = (q_rope * scale).astype(BF16)
        k_ref[:, h * QK_PAD:h * QK_PAD + LANES] = kn[:, h * MLA_NOPE:(h + 1) * MLA_NOPE].astype(BF16)
        k_ref[:, h * QK_PAD + LANES:(h + 1) * QK_PAD] = k_rope_b


def _half_rotation(w):
    half = w.shape[-1] // 2
    return jnp.concatenate([-w[..., half:], w[..., :half]], axis=-1)


def _pad_lanes(w):
    return jnp.pad(w, [(0, 0)] * (w.ndim - 1) + [(0, LANES - w.shape[-1])])


def _mla_proj(x2d, ang, w_in, q_norm, kv_norm, w_uq, w_uk, w_uv):
    t, d = x2d.shape
    tm = min(ROW_BLOCK, t)
    kv_end = MLA_Q_RANK + MLA_KV_RANK
    w_rope = w_in[:, kv_end:]
    w_in_ext = jnp.concatenate(
        [w_in[:, :kv_end], _pad_lanes(w_rope), _pad_lanes(_half_rotation(w_rope))], axis=1).astype(BF16)
    wq = w_uq.reshape(MLA_Q_RANK, MLA_HEADS, MLA_NOPE + MLA_ROPE)
    wq_rope = wq[..., MLA_NOPE:]
    w_uq_ext = jnp.concatenate(
        [wq[..., :MLA_NOPE], _pad_lanes(wq_rope), _pad_lanes(_half_rotation(wq_rope))],
        axis=-1).reshape(MLA_Q_RANK, MLA_HEADS * 3 * LANES).astype(BF16)
    scale = (MLA_NOPE + MLA_ROPE) ** -0.5
    n_in = w_in_ext.shape[1]
    return pl.pallas_call(
        functools.partial(_mla_proj_kernel, scale=scale),
        grid=(t // tm,),
        in_specs=[
            pl.BlockSpec((tm, d), lambda i: (i, 0)),
            pl.BlockSpec((tm, LANES), lambda i: (i, 0)),
            _full((d, n_in)), _full((1, MLA_Q_RANK)), _full((1, MLA_KV_RANK)),
            _full((MLA_Q_RANK, MLA_HEADS * 3 * LANES)),
            _full((MLA_KV_RANK, MLA_HEADS * MLA_NOPE)), _full((MLA_KV_RANK, MLA_HEADS * MLA_V)),
        ],
        out_specs=[
            pl.BlockSpec((tm, MLA_HEADS * QK_PAD), lambda i: (i, 0)),
            pl.BlockSpec((tm, MLA_HEADS * QK_PAD), lambda i: (i, 0)),
            pl.BlockSpec((tm, MLA_HEADS * MLA_V), lambda i: (i, 0)),
        ],
        out_shape=[
            jax.ShapeDtypeStruct((t, MLA_HEADS * QK_PAD), BF16),
            jax.ShapeDtypeStruct((t, MLA_HEADS * QK_PAD), BF16),
            jax.ShapeDtypeStruct((t, MLA_HEADS * MLA_V), BF16),
        ],
        compiler_params=_params(1),
        name="mla_proj",
    )(x2d, ang, w_in_ext, q_norm.reshape(1, -1), kv_norm.reshape(1, -1), w_uq_ext,
      w_uk.astype(BF16), w_uv.astype(BF16))


def _attn_kernel(q_ref, k_ref, v_ref, o_ref):
    tq = q_ref.shape[0]
    qi = pl.program_id(2)
    q = q_ref[...]

    def step(k, v, carry, mask):
        m, l, acc = carry
        s = lax.dot_general(q, k, NT_DIMS, preferred_element_type=F32)
        if mask is not None:
            s = jnp.where(mask, s, jnp.finfo(F32).min)
        m_new = jnp.maximum(m, jnp.max(s, axis=-1, keepdims=True))
        corr = jnp.exp(m - m_new)
        p = jnp.exp(s - m_new)
        l = corr * l + jnp.sum(p, axis=-1, keepdims=True)
        acc = corr * acc + jnp.dot(p.astype(BF16), v, preferred_element_type=F32)
        return m_new, l, acc

    def body(kb, carry):
        r0 = pl.multiple_of(kb * tq, tq)
        return step(k_ref[pl.ds(r0, tq), :], v_ref[pl.ds(r0, tq), :], carry, None)

    init = (jnp.full((tq, 1), -jnp.inf, F32), jnp.zeros((tq, 1), F32), jnp.zeros((tq, MLA_V), F32))
    carry = lax.fori_loop(0, qi, body, init)
    chunk_bits = CHUNK.bit_length() - 1
    q_chunk = lax.shift_right_logical(lax.broadcasted_iota(jnp.int32, (tq, tq), 0), chunk_bits)
    k_chunk = lax.shift_right_logical(lax.broadcasted_iota(jnp.int32, (tq, tq), 1), chunk_bits)
    r0 = pl.multiple_of(qi * tq, tq)
    _, l, acc = step(k_ref[pl.ds(r0, tq), :], v_ref[pl.ds(r0, tq), :], carry, k_chunk <= q_chunk)
    o_ref[...] = (acc / l).astype(BF16)


def _attention(q, k, v, batch, seq):
    t = q.shape[0]
    tq = min(ROW_BLOCK, seq)
    nq = seq // tq
    return pl.pallas_call(
        _attn_kernel,
        grid=(batch, MLA_HEADS, nq),
        in_specs=[
            pl.BlockSpec((tq, QK_PAD), lambda b, h, i: (b * nq + i, h)),
            pl.BlockSpec((seq, QK_PAD), lambda b, h, i: (b, h)),
            pl.BlockSpec((seq, MLA_V), lambda b, h, i: (b, h)),
        ],
        out_specs=pl.BlockSpec((tq, MLA_V), lambda b, h, i: (b * nq + i, h)),
        out_shape=jax.ShapeDtypeStruct((t, MLA_HEADS * MLA_V), BF16),
        compiler_params=_params(3),
        name="mla_attention",
    )(q, k, v)


def _out_proj_kernel(o_ref, x_ref, w_ref, g_ref, b_ref, y_ref, *, alpha):
    mix = jnp.dot(o_ref[...], w_ref[...], preferred_element_type=F32)
    y_ref[...] = _layer_norm(alpha * x_ref[...] + mix, g_ref[...], b_ref[...])


def _out_proj(o, x2d, w_out, g, b, alpha):
    t, d = x2d.shape
    tm = min(ROW_BLOCK, t)
    return pl.pallas_call(
        functools.partial(_out_proj_kernel, alpha=alpha),
        grid=(t // tm,),
        in_specs=[
            pl.BlockSpec((tm, o.shape[1]), lambda i: (i, 0)),
            pl.BlockSpec((tm, d), lambda i: (i, 0)),
            _full(w_out.shape), _full((1, d)), _full((1, d)),
        ],
        out_specs=pl.BlockSpec((tm, d), lambda i: (i, 0)),
        out_shape=jax.ShapeDtypeStruct((t, d), F32),
        compiler_params=_params(1),
        name="mla_out_proj",
    )(o, x2d, w_out.astype(BF16), g.reshape(1, -1), b.reshape(1, -1))


def _mla_layer(x2d, ang, batch, seq, w_in, q_norm, kv_norm, w_uq, w_uk, w_uv, w_out, l1g, l1b, alpha):
    q, k, v = _mla_proj(x2d, ang, w_in, q_norm, kv_norm, w_uq, w_uk, w_uv)
    o = _attention(q, k, v, batch, seq)
    return _out_proj(o, x2d, w_out, l1g, l1b, alpha)


def _top_k_ranked(s):
    n = s.shape[0]
    key_id = lax.broadcasted_iota(jnp.int32, s.shape, 0)
    rank = jnp.full(s.shape, float(PEER_TOPK), F32)
    vals = []
    for r in range(PEER_TOPK):
        m = jnp.max(s, axis=0, keepdims=True)
        first = jnp.min(jnp.where(s == m, key_id, n), axis=0, keepdims=True)
        hit = key_id == first
        rank = jnp.where(hit, float(r), rank)
        s = jnp.where(hit, -jnp.inf, s)
        vals.append(m)
    return jnp.concatenate(vals, axis=0), rank


def _merge_top_k(sv1, sv2):
    row_id = lax.broadcasted_iota(jnp.int32, sv1.shape, 0).astype(F32)
    count = jnp.zeros(sv1.shape, F32)
    front = sv1 + sv2[0:1]
    picked = []
    for r in range(PEER_TOPK):
        m = jnp.max(front, axis=0, keepdims=True)
        picked.append(m)
        first = jnp.min(jnp.where(front == m, row_id, float(PEER_TOPK)), axis=0, keepdims=True)
        hit = row_id == first
        count = jnp.where(hit, count + 1.0, count)
        if r + 1 < PEER_TOPK:
            new_len = jnp.sum(jnp.where(hit, count, 0.0), axis=0, keepdims=True)
            a_val = jnp.sum(jnp.where(hit, sv1, 0.0), axis=0, keepdims=True)
            b_val = jnp.sum(jnp.where(row_id == new_len, sv2, 0.0), axis=0, keepdims=True)
            nxt = jnp.where(new_len >= float(PEER_TOPK), -jnp.inf, a_val + b_val)
            front = jnp.where(hit, nxt, front)
    return count, picked


def _top_k_no_ties(s):
    rank = jnp.full(s.shape, float(PEER_TOPK), F32)
    vals = []
    for r in range(PEER_TOPK):
        m = jnp.max(s, axis=0, keepdims=True)
        hit = s == m
        rank = jnp.where(hit, float(r), rank)
        s = jnp.where(hit, -jnp.inf, s)
        vals.append(m)
    members = jnp.sum(jnp.where(rank < float(PEER_TOPK), 1.0, 0.0), axis=0, keepdims=True)
    return vals, rank, members


def _merge_no_ties(v1, v2):
    sub = 8
    lo2 = jnp.concatenate(v2[:sub], axis=0)
    hi2 = jnp.concatenate(v2[sub:], axis=0)
    hi1 = jnp.concatenate(v1[sub:], axis=0)
    row = lax.broadcasted_iota(jnp.int32, lo2.shape, 0)
    cand, valid = [], []
    for a in range(sub):
        lim = min(PEER_TOPK // (a + 1), sub)
        ok = row < lim
        cand.append(jnp.where(ok, v1[a] + lo2, -jnp.inf))
        valid.append(ok)
    cand.append(v1[0] + hi2)
    cand.append(hi1 + v2[0])
    valid += [None, None]
    picked = []
    for r in range(PEER_TOPK):
        m = cand[0]
        for c in cand[1:]:
            m = jnp.maximum(m, c)
        m = jnp.max(m, axis=0, keepdims=True)
        picked.append(m)
        cand = [jnp.where(c == m, -jnp.inf, c) for c in cand]
    taken = []
    for c, ok in zip(cand, valid):
        t = c == -jnp.inf
        taken.append(jnp.where(t if ok is None else t & ok, 1.0, 0.0))
    counts = [jnp.sum(t, axis=0, keepdims=True) for t in taken[:sub]]
    counts[0] = counts[0] + jnp.sum(taken[sub], axis=0, keepdims=True)
    counts += [taken[sub + 1][a:a + 1] for a in range(sub)]
    total = counts[0]
    for cnt in counts[1:]:
        total = total + cnt
    return counts, picked, total


def _route_kernel(x_ref, wq_ref, keys_ref, rank2_ref, e2_ref, len1_ref, e1_ref, s_scr):
    tb = x_ref.shape[0]
    n_chunks = tb // LANES
    half = PEER_D_QUERY // 2
    xb = x_ref[...].astype(BF16)
    q_t = lax.dot_general(wq_ref[...], xb, NT_DIMS, preferred_element_type=F32)
    for h in range(PEER_HEADS):
        for p in range(2):
            r0 = h * PEER_D_QUERY + p * half
            s = jnp.dot(keys_ref[p], q_t[r0:r0 + half, :].astype(BF16), preferred_element_type=F32)
            for c in range(n_chunks):
                s_scr[2 * h + p, c] = s[:, c * LANES:(c + 1) * LANES]

    def emit(h, c, s1, s2, rank1, rank2, top1, top2, counts, picked):
        z = jnp.ones_like(picked[0])
        for r in range(1, PEER_TOPK):
            z = z + jnp.exp(picked[r] - picked[0])
        len1 = jnp.zeros(s1.shape, F32)
        for a in range(PEER_TOPK):
            len1 = jnp.where(rank1 == float(a), counts[a], len1)
        rank2_ref[h, c] = rank2
        e2_ref[h, c] = jnp.exp(s2 - top2)
        len1_ref[h, c] = len1
        e1_ref[h, c] = jnp.exp(s1 - top1) * (0.5 / z)

    group = min(4, n_chunks)

    def body(idx, carry):
        h = idx // (n_chunks // group)
        c0 = (idx % (n_chunks // group)) * group
        flags = []
        for dc in range(group):
            c = c0 + dc
            s1 = s_scr[2 * h, c]
            s2 = s_scr[2 * h + 1, c]
            v1, rank1, n1 = _top_k_no_ties(s1)
            v2, rank2, n2 = _top_k_no_ties(s2)
            counts, picked, n12 = _merge_no_ties(v1, v2)
            emit(h, c, s1, s2, rank1, rank2, v1[0], v2[0], counts, picked)
            k = float(PEER_TOPK)
            flags.append(jnp.abs(n1 - k) + jnp.abs(n2 - k) + jnp.abs(n12 - k))

        def redo(c):
            s1 = s_scr[2 * h, c]
            s2 = s_scr[2 * h + 1, c]
            sv1, rank1 = _top_k_ranked(s1)
            sv2, rank2 = _top_k_ranked(s2)
            count, picked = _merge_top_k(sv1, sv2)
            counts = [count[a:a + 1] for a in range(PEER_TOPK)]
            emit(h, c, s1, s2, rank1, rank2, sv1[0:1], sv2[0:1], counts, picked)

        @pl.when(jnp.max(sum(flags)) > 0.0)
        def _():
            for dc in range(group):
                pl.when(jnp.max(flags[dc]) > 0.0)(functools.partial(redo, c0 + dc))

        return carry

    lax.fori_loop(0, PEER_HEADS * n_chunks // group, body, 0)


def _route(x2d, w_q, sub_keys):
    t, d = x2d.shape
    tb = min(ROW_BLOCK, t)
    nc = tb // LANES
    table_shape = (PEER_HEADS, t // LANES, PEER_N_KEYS, LANES)
    table_spec = pl.BlockSpec((PEER_HEADS, nc, PEER_N_KEYS, LANES), lambda i: (0, i, 0, 0))
    return pl.pallas_call(
        _route_kernel,
        grid=(t // tb,),
        in_specs=[
            pl.BlockSpec((tb, d), lambda i: (i, 0)),
            _full((PEER_HEADS * PEER_D_QUERY, d)),
            _full((2, PEER_N_KEYS, PEER_D_QUERY // 2)),
        ],
        out_specs=[table_spec] * 4,
        out_shape=[jax.ShapeDtypeStruct(table_shape, F32)] * 4,
        scratch_shapes=[pltpu.VMEM((2 * PEER_HEADS, nc, PEER_N_KEYS, LANES), F32)],
        compiler_params=_params(1),
        name="peer_route",
    )(x2d, w_q.T.astype(BF16), sub_keys.astype(BF16))


def _pow2_scale(bound):
    bound = jnp.maximum(bound, jnp.finfo(F32).tiny)
    return jnp.exp2(jnp.floor(jnp.log2(FP8_TARGET / bound)))


def _abs_max(x):
    return jnp.max(jnp.max(jnp.abs(x), axis=0, keepdims=True), axis=1, keepdims=True)


def _expert_kernel(x_ref, u_ref, vt_ref, wscale_ref, rank2_ref, e2_ref, len1_ref, e1_ref, g_ref,
                   b_ref, o_ref, xt_scr, h_scr, a_scr, acc_scr, rank2_scr, e2_scr, inv_scr, *,
                   alpha, n_tiles):
    tb = x_ref.shape[0]
    te = u_ref.shape[0]
    slabs = te // PEER_N_KEYS
    n_chunks = tb // LANES
    pack = 16
    j = pl.program_id(1)

    @pl.when(j == 0)
    def _():
        x = x_ref[...]
        s_x = _pow2_scale(_abs_max(x))
        xt_scr[...] = (x * s_x).T.astype(F8)
        inv_scr[...] = jnp.broadcast_to(wscale_ref[0:1, :] / s_x, inv_scr.shape)
        acc_scr[...] = jnp.zeros(acc_scr.shape, F32)
        for h in range(PEER_HEADS):
            for c in range(n_chunks):
                rank2_scr[h, c] = rank2_ref[h, c].astype(BF16)
                e2_scr[h, c] = e2_ref[h, c].astype(BF16)

    h = jnp.dot(u_ref[...], xt_scr[...], preferred_element_type=F32)
    h_scr[...] = h
    inv_h = inv_scr[0:1, :]
    s_a = _pow2_scale(_abs_max(h) * inv_h * float(PEER_HEADS))

    def slab(s):
        i1 = j * slabs + s
        r0 = s * PEER_N_KEYS
        for c in range(n_chunks):
            lanes = slice(c * LANES, (c + 1) * LANES)
            length, e1 = [], []
            for h in range(PEER_HEADS):
                row = len1_ref[h, c, pl.ds(i1, 1), :]
                length.append(jnp.broadcast_to(row, (pack, LANES)).astype(BF16))
                row = e1_ref[h, c, pl.ds(i1, 1), :] * s_a
                e1.append(jnp.broadcast_to(row, (pack, LANES)).astype(BF16))
            outs = []
            for k in range(PEER_N_KEYS // pack):
                rows = slice(k * pack, (k + 1) * pack)
                gate = None
                for h in range(PEER_HEADS):
                    term = jnp.where(rank2_scr[h, c, rows, :] < length[h],
                                     e2_scr[h, c, rows, :], jnp.zeros((), BF16)) * e1[h]
                    gate = term if gate is None else gate + term
                pre = h_scr[r0 + k * pack:r0 + (k + 1) * pack, lanes] * inv_h
                act = pre * (1.0 + lax.erf(pre * (2.0 ** -0.5)))
                outs.append(act.astype(BF16) * gate)
                if k % 2 == 1:
                    both = jnp.concatenate(outs, axis=0)
                    outs = []
                    a_scr[r0 + (k - 1) * pack:r0 + (k + 1) * pack, lanes] = both.astype(F8)

    @pl.when(j >= 0)
    def _():
        for s in range(slabs):
            slab(s)

    y = jnp.dot(vt_ref[...], a_scr[...], preferred_element_type=F32)
    acc_scr[...] += y * (wscale_ref[1:2, 0:1] / s_a[:, 0:1])

    @pl.when(j == n_tiles - 1)
    def _():
        y = acc_scr[...].T
        o_ref[...] = _layer_norm(alpha * x_ref[...] + y, g_ref[...], b_ref[...])


def _experts(x2d, tables, u, v, g, b, alpha):
    t, d = x2d.shape
    n_experts = u.shape[0]
    tb = min(ROW_BLOCK, t)
    te = EXPERT_TILE
    n_tiles = n_experts // te
    nc = tb // LANES
    table_spec = pl.BlockSpec((PEER_HEADS, nc, PEER_N_KEYS, LANES), lambda i, j: (0, i, 0, 0))
    s_u = _pow2_scale(jnp.max(jnp.abs(u)))
    s_v = _pow2_scale(jnp.max(jnp.abs(v)))
    wscale = jnp.broadcast_to(jnp.stack([1.0 / s_u, 1.0 / s_v])[:, None], (2, LANES))
    return pl.pallas_call(
        functools.partial(_expert_kernel, alpha=alpha, n_tiles=n_tiles),
        grid=(t // tb, n_tiles),
        in_specs=[
            pl.BlockSpec((tb, d), lambda i, j: (i, 0)),
            pl.BlockSpec((te, d), lambda i, j: (j, 0)),
            pl.BlockSpec((d, te), lambda i, j: (0, j)),
            pl.BlockSpec((2, LANES), lambda i, j: (0, 0)),
            table_spec, table_spec, table_spec, table_spec,
            pl.BlockSpec((1, d), lambda i, j: (0, 0)),
            pl.BlockSpec((1, d), lambda i, j: (0, 0)),
        ],
        out_specs=pl.BlockSpec((tb, d), lambda i, j: (i, 0)),
        out_shape=jax.ShapeDtypeStruct((t, d), F32),
        scratch_shapes=[
            pltpu.VMEM((d, tb), F8),
            pltpu.VMEM((te, tb), F32),
            pltpu.VMEM((te, tb), F8),
            pltpu.VMEM((d, tb), F32),
            pltpu.VMEM((PEER_HEADS, nc, PEER_N_KEYS, LANES), BF16),
            pltpu.VMEM((PEER_HEADS, nc, PEER_N_KEYS, LANES), BF16),
            pltpu.VMEM((8, LANES), F32),
        ],
        compiler_params=_params(2),
        name="peer_experts",
    )(x2d, (u * s_u).astype(F8), (v.T * s_v).astype(F8), wscale, *tables,
      g.reshape(1, -1), b.reshape(1, -1))


def _peer_layer(x2d, w_q, sub_keys, u, v, g, b, alpha):
    tables = _route(x2d, w_q, sub_keys)
    return _experts(x2d, tables, u, v, g, b, alpha)


def kernel(x, positions, conv_w_in, conv_b_in, conv_w_dw, conv_b_dw, conv_ln_g, conv_ln_b,
           conv_w_out, conv_b_out, mla_w_in, mla_q_norm, mla_kv_norm, mla_w_uq, mla_w_uk,
           mla_w_uv, mla_w_out, peer_w_q, peer_sub_keys, peer_u, peer_v,
           ln1_g, ln1_b, ln2_g, ln2_b):
    batch, seq, d = x.shape
    depth = ln1_g.shape[0]
    alpha = (2 * depth) ** 0.25
    x2d = x.reshape(batch * seq, d)

    inv = 1.0 / (ROPE_THETA ** (jnp.arange(0, MLA_ROPE, 2, dtype=F32) / MLA_ROPE))
    ang = positions.astype(F32).reshape(batch * seq, 1) * inv
    ang = _pad_lanes(jnp.concatenate([ang, ang], axis=-1))

    for i in range(depth):
        j = i // 2
        if i % 2 == 0:
            x2d = _conv_layer(x2d, seq, conv_w_in[j], conv_b_in[j], conv_w_dw[j], conv_b_dw[j],
                              conv_ln_g[j], conv_ln_b[j], conv_w_out[j], conv_b_out[j],
                              ln1_g[i], ln1_b[i], alpha)
        else:
            x2d = _mla_layer(x2d, ang, batch, seq, mla_w_in[j], mla_q_norm[j], mla_kv_norm[j],
                             mla_w_uq[j], mla_w_uk[j], mla_w_uv[j], mla_w_out[j],
                             ln1_g[i], ln1_b[i], alpha)
        x2d = _peer_layer(x2d, peer_w_q[i], peer_sub_keys[i], peer_u[i], peer_v[i],
                          ln2_g[i], ln2_b[i], alpha)
    return x2d.reshape(batch, seq, d)
```

```python
import functools

import jax
import jax.numpy as jnp
from jax import lax
from jax.experimental import pallas as pl
from jax.experimental.pallas import tpu as pltpu

F32 = jnp.float32
BF16 = jnp.bfloat16
F8 = jnp.float8_e4m3fn

LANES = 128
VMEM_LIMIT = 58 * 1024 * 1024

D_MODEL = 1024
CHUNK = 64
CONV_WIDTH = 31
CONV_HALO = 32

MLA_HEADS = 8
MLA_NOPE = 128
MLA_ROPE = 64
MLA_V = 128
MLA_Q_RANK = 384
MLA_KV_RANK = 256
ROPE_THETA = 10000.0
QK_PAD = 256
ATTN_HEADS_PER_STEP = 2

PEER_HEADS = 8
PEER_N_KEYS = 128
PEER_D_QUERY = 256
PEER_TOPK = 16

LN_EPS = 1e-5
RMS_EPS = 1e-6

ROW_BLOCK = 512
EXPERT_TILE = 2048
FP8_TARGET = 224.0
FP8_MIN_BOUND = 1e-30

NT_DIMS = (((1,), (1,)), ((), ()))


def _params(n_axes, flags=None):
    return pltpu.CompilerParams(
        dimension_semantics=("arbitrary",) * n_axes, vmem_limit_bytes=VMEM_LIMIT, flags=flags)


def _layer_norm(x, g, b):
    mu = jnp.mean(x, axis=-1, keepdims=True)
    xc = x - mu
    var = jnp.mean(xc * xc, axis=-1, keepdims=True)
    return xc * lax.rsqrt(var + LN_EPS) * g + b


def _rms_norm(x, g):
    return x * lax.rsqrt(jnp.mean(x * x, axis=-1, keepdims=True) + RMS_EPS) * g


def _full(shape):
    return pl.BlockSpec(shape, lambda *_: (0,) * len(shape))


def _conv_layer_kernel(x_ref, w_in_ref, b_in_ref, w_dw_ref, b_dw_ref, lng_ref, lnb_ref,
                       w_out_ref, b_out_ref, l1g_ref, l1b_ref, o_ref, gbuf, cbuf, *,
                       blocks_per_seq, alpha):
    tm, d = x_ref.shape
    i = pl.program_id(0)

    @pl.when(i % blocks_per_seq == 0)
    def _():
        gbuf[0:CONV_HALO, :] = jnp.zeros((CONV_HALO, d), F32)

    x = x_ref[...]
    h = jnp.dot(x.astype(BF16), w_in_ref[...], preferred_element_type=F32) + b_in_ref[...]
    gbuf[CONV_HALO:CONV_HALO + tm, :] = h[:, :d] * jax.nn.sigmoid(h[:, d:])

    base = CONV_HALO - (CONV_WIDTH - 1)
    sublanes = 8
    taps_by_shift = {}
    for k in range(CONV_WIDTH):
        taps_by_shift.setdefault((base + k) % sublanes, []).append((k, (base + k) // sublanes))
    rc, lc = min(128, tm), 256
    for r0 in range(0, tm, rc):
        for l0 in range(0, d, lc):
            lanes = slice(l0, l0 + lc)
            acc = jnp.broadcast_to(b_dw_ref[:, lanes], (rc, lc))
            for shift, taps in taps_by_shift.items():
                span = sublanes * max(m for _, m in taps) + rc
                if shift:
                    n = span + sublanes
                    win = pltpu.roll(gbuf[r0:r0 + n, lanes], n - shift, axis=0)
                else:
                    win = gbuf[r0:r0 + span, lanes]
                for k, m in taps:
                    acc = acc + w_dw_ref[k:k + 1, lanes] * win[sublanes * m:sublanes * m + rc]
            cbuf[r0:r0 + rc, lanes] = acc
    gbuf[0:CONV_HALO, :] = gbuf[tm:tm + CONV_HALO, :]

    y = _layer_norm(cbuf[...], lng_ref[...], lnb_ref[...])
    y = y * jax.nn.sigmoid(y)
    mix = jnp.dot(y.astype(BF16), w_out_ref[...], preferred_element_type=F32) + b_out_ref[...]
    o_ref[...] = _layer_norm(alpha * x + mix, l1g_ref[...], l1b_ref[...])


def _conv_layer(x2d, seq, w_in, b_in, w_dw, b_dw, ln_g, ln_b, w_out, b_out, l1g, l1b, alpha):
    t, d = x2d.shape
    tm = min(ROW_BLOCK, seq)
    row = lambda v: v.reshape(1, -1)
    return pl.pallas_call(
        functools.partial(_conv_layer_kernel, blocks_per_seq=seq // tm, alpha=alpha),
        grid=(t // tm,),
        in_specs=[
            pl.BlockSpec((tm, d), lambda i: (i, 0)),
            _full((d, 2 * d)), _full((1, 2 * d)), _full((CONV_WIDTH, d)), _full((1, d)),
            _full((1, d)), _full((1, d)), _full((d, d)), _full((1, d)), _full((1, d)),
            _full((1, d)),
        ],
        out_specs=pl.BlockSpec((tm, d), lambda i: (i, 0)),
        out_shape=jax.ShapeDtypeStruct((t, d), F32),
        scratch_shapes=[pltpu.VMEM((CONV_HALO + tm, d), F32), pltpu.VMEM((tm, d), F32)],
        compiler_params=_params(1),
        name="conv_layer",
    )(x2d, w_in.astype(BF16), row(b_in), w_dw, row(b_dw), row(ln_g), row(ln_b),
      w_out.astype(BF16), row(b_out), row(l1g), row(l1b))


def _mla_proj_kernel(x_ref, ang_ref, w_in_ref, qn_ref, kvn_ref, w_uq_ref, w_uk_ref, w_uv_ref,
                     q_ref, k_ref, v_ref, *, scale):
    x = x_ref[...].astype(BF16)
    c = jnp.dot(x, w_in_ref[...], preferred_element_type=F32)
    cq = _rms_norm(c[:, :MLA_Q_RANK], qn_ref[...])
    kv_end = MLA_Q_RANK + MLA_KV_RANK
    ckv = _rms_norm(c[:, MLA_Q_RANK:kv_end], kvn_ref[...])
    ang = ang_ref[...]
    cos, sin = jnp.cos(ang), jnp.sin(ang)
    k_rope = c[:, kv_end:kv_end + LANES] * cos + c[:, kv_end + LANES:kv_end + 2 * LANES] * sin

    q = jnp.dot(cq.astype(BF16), w_uq_ref[...], preferred_element_type=F32)
    ckv_b = ckv.astype(BF16)
    kn = jnp.dot(ckv_b, w_uk_ref[...], preferred_element_type=F32)
    v_ref[...] = jnp.dot(ckv_b, w_uv_ref[...], preferred_element_type=F32).astype(BF16)
    k_rope_b = k_rope.astype(BF16)
    for h in range(MLA_HEADS):
        qh = q[:, h * 3 * LANES:(h + 1) * 3 * LANES]
        q_rope = qh[:, LANES:2 * LANES] * cos + qh[:, 2 * LANES:] * sin
        q_ref[:, h * QK_PAD:h * QK_PAD + LANES] = (qh[:, :LANES] * scale).astype(BF16)
        q_ref[:, h * QK_PAD + LANES:(h + 1) * QK_PAD] = (q_rope * scale).astype(BF16)
        k_ref[:, h * QK_PAD:h * QK_PAD + LANES] = kn[:, h * MLA_NOPE:(h + 1) * MLA_NOPE].astype(BF16)
        k_ref[:, h * QK_PAD + LANES:(h + 1) * QK_PAD] = k_rope_b


def _half_rotation(w):
    half = w.shape[-1] // 2
    return jnp.concatenate([-w[..., half:], w[..., :half]], axis=-1)


def _pad_lanes(w):
    return jnp.pad(w, [(0, 0)] * (w.ndim - 1) + [(0, LANES - w.shape[-1])])


def _mla_proj(x2d, ang, w_in, q_norm, kv_norm, w_uq, w_uk, w_uv):
    t, d = x2d.shape
    tm = min(ROW_BLOCK, t)
    kv_end = MLA_Q_RANK + MLA_KV_RANK
    w_rope = w_in[:, kv_end:]
    w_in_ext = jnp.concatenate(
        [w_in[:, :kv_end], _pad_lanes(w_rope), _pad_lanes(_half_rotation(w_rope))], axis=1).astype(BF16)
    wq = w_uq.reshape(MLA_Q_RANK, MLA_HEADS, MLA_NOPE + MLA_ROPE)
    wq_rope = wq[..., MLA_NOPE:]
    w_uq_ext = jnp.concatenate(
        [wq[..., :MLA_NOPE], _pad_lanes(wq_rope), _pad_lanes(_half_rotation(wq_rope))],
        axis=-1).reshape(MLA_Q_RANK, MLA_HEADS * 3 * LANES).astype(BF16)
    scale = (MLA_NOPE + MLA_ROPE) ** -0.5
    n_in = w_in_ext.shape[1]
    return pl.pallas_call(
        functools.partial(_mla_proj_kernel, scale=scale),
        grid=(t // tm,),
        in_specs=[
            pl.BlockSpec((tm, d), lambda i: (i, 0)),
            pl.BlockSpec((tm, LANES), lambda i: (i, 0)),
            _full((d, n_in)), _full((1, MLA_Q_RANK)), _full((1, MLA_KV_RANK)),
            _full((MLA_Q_RANK, MLA_HEADS * 3 * LANES)),
            _full((MLA_KV_RANK, MLA_HEADS * MLA_NOPE)), _full((MLA_KV_RANK, MLA_HEADS * MLA_V)),
        ],
        out_specs=[
            pl.BlockSpec((tm, MLA_HEADS * QK_PAD), lambda i: (i, 0)),
            pl.BlockSpec((tm, MLA_HEADS * QK_PAD), lambda i: (i, 0)),
            pl.BlockSpec((tm, MLA_HEADS * MLA_V), lambda i: (i, 0)),
        ],
        out_shape=[
            jax.ShapeDtypeStruct((t, MLA_HEADS * QK_PAD), BF16),
            jax.ShapeDtypeStruct((t, MLA_HEADS * QK_PAD), BF16),
            jax.ShapeDtypeStruct((t, MLA_HEADS * MLA_V), BF16),
        ],
        compiler_params=_params(1),
        name="mla_proj",
    )(x2d, ang, w_in_ext, q_norm.reshape(1, -1), kv_norm.reshape(1, -1), w_uq_ext,
      w_uk.astype(BF16), w_uv.astype(BF16))


def _attn_kernel(q_ref, k_ref, v_ref, o_ref):
    tq = q_ref.shape[0]
    qi = pl.program_id(2)
    heads = range(q_ref.shape[1] // QK_PAD)
    q = [q_ref[:, h * QK_PAD:(h + 1) * QK_PAD] for h in heads]

    def step(h, r0, carry, mask):
        m, l, acc = carry
        k = k_ref[pl.ds(r0, tq), h * QK_PAD:(h + 1) * QK_PAD]
        v = v_ref[pl.ds(r0, tq), h * MLA_V:(h + 1) * MLA_V]
        s = lax.dot_general(q[h], k, NT_DIMS, preferred_element_type=F32)
        if mask is not None:
            s = jnp.where(mask, s, jnp.finfo(F32).min)
        m_new = jnp.maximum(m, jnp.max(s, axis=-1, keepdims=True))
        corr = jnp.exp(m - m_new)
        p = jnp.exp(s - m_new)
        l = corr * l + jnp.sum(p, axis=-1, keepdims=True)
        acc = corr * acc + jnp.dot(p.astype(BF16), v, preferred_element_type=F32)
        return m_new, l, acc

    def body(kb, carries):
        r0 = pl.multiple_of(kb * tq, tq)
        return tuple(step(h, r0, carries[h], None) for h in heads)

    init = (jnp.full((tq, 1), -jnp.inf, F32), jnp.zeros((tq, 1), F32), jnp.zeros((tq, MLA_V), F32))
    carries = lax.fori_loop(0, qi, body, tuple(init for _ in heads))
    chunk_bits = CHUNK.bit_length() - 1
    q_chunk = lax.shift_right_logical(lax.broadcasted_iota(jnp.int32, (tq, tq), 0), chunk_bits)
    k_chunk = lax.shift_right_logical(lax.broadcasted_iota(jnp.int32, (tq, tq), 1), chunk_bits)
    r0 = pl.multiple_of(qi * tq, tq)
    for h in heads:
        _, l, acc = step(h, r0, carries[h], k_chunk <= q_chunk)
        o_ref[:, h * MLA_V:(h + 1) * MLA_V] = (acc / l).astype(BF16)


def _attention(q, k, v, batch, seq):
    t = q.shape[0]
    tq = min(ROW_BLOCK, seq)
    nq = seq // tq
    hp = ATTN_HEADS_PER_STEP
    return pl.pallas_call(
        _attn_kernel,
        grid=(batch, MLA_HEADS // hp, nq),
        in_specs=[
            pl.BlockSpec((tq, hp * QK_PAD), lambda b, h, i: (b * nq + i, h)),
            pl.BlockSpec((seq, hp * QK_PAD), lambda b, h, i: (b, h)),
            pl.BlockSpec((seq, hp * MLA_V), lambda b, h, i: (b, h)),
        ],
        out_specs=pl.BlockSpec((tq, hp * MLA_V), lambda b, h, i: (b * nq + i, h)),
        out_shape=jax.ShapeDtypeStruct((t, MLA_HEADS * MLA_V), BF16),
        compiler_params=_params(3),
        name="mla_attention",
    )(q, k, v)


def _out_proj_kernel(o_ref, x_ref, w_ref, g_ref, b_ref, y_ref, *, alpha):
    mix = jnp.dot(o_ref[...], w_ref[...], preferred_element_type=F32)
    y_ref[...] = _layer_norm(alpha * x_ref[...] + mix, g_ref[...], b_ref[...])


def _out_proj(o, x2d, w_out, g, b, alpha):
    t, d = x2d.shape
    tm = min(ROW_BLOCK, t)
    return pl.pallas_call(
        functools.partial(_out_proj_kernel, alpha=alpha),
        grid=(t // tm,),
        in_specs=[
            pl.BlockSpec((tm, o.shape[1]), lambda i: (i, 0)),
            pl.BlockSpec((tm, d), lambda i: (i, 0)),
            _full(w_out.shape), _full((1, d)), _full((1, d)),
        ],
        out_specs=pl.BlockSpec((tm, d), lambda i: (i, 0)),
        out_shape=jax.ShapeDtypeStruct((t, d), F32),
        compiler_params=_params(1),
        name="mla_out_proj",
    )(o, x2d, w_out.astype(BF16), g.reshape(1, -1), b.reshape(1, -1))


def _mla_layer(x2d, ang, batch, seq, w_in, q_norm, kv_norm, w_uq, w_uk, w_uv, w_out, l1g, l1b, alpha):
    q, k, v = _mla_proj(x2d, ang, w_in, q_norm, kv_norm, w_uq, w_uk, w_uv)
    o = _attention(q, k, v, batch, seq)
    return _out_proj(o, x2d, w_out, l1g, l1b, alpha)


def _top_k_ranked(s):
    n = s.shape[0]
    key_id = lax.broadcasted_iota(jnp.int32, s.shape, 0)
    rank = jnp.full(s.shape, float(PEER_TOPK), F32)
    vals = []
    for r in range(PEER_TOPK):
        m = jnp.max(s, axis=0, keepdims=True)
        first = jnp.min(jnp.where(s == m, key_id, n), axis=0, keepdims=True)
        hit = key_id == first
        rank = jnp.where(hit, float(r), rank)
        s = jnp.where(hit, -jnp.inf, s)
        vals.append(m)
    return jnp.concatenate(vals, axis=0), rank


def _merge_top_k(sv1, sv2):
    row_id = lax.broadcasted_iota(jnp.int32, sv1.shape, 0).astype(F32)
    count = jnp.zeros(sv1.shape, F32)
    front = sv1 + sv2[0:1]
    picked = []
    for r in range(PEER_TOPK):
        m = jnp.max(front, axis=0, keepdims=True)
        picked.append(m)
        first = jnp.min(jnp.where(front == m, row_id, float(PEER_TOPK)), axis=0, keepdims=True)
        hit = row_id == first
        count = jnp.where(hit, count + 1.0, count)
        if r + 1 < PEER_TOPK:
            new_len = jnp.sum(jnp.where(hit, count, 0.0), axis=0, keepdims=True)
            a_val = jnp.sum(jnp.where(hit, sv1, 0.0), axis=0, keepdims=True)
            b_val = jnp.sum(jnp.where(row_id == new_len, sv2, 0.0), axis=0, keepdims=True)
            nxt = jnp.where(new_len >= float(PEER_TOPK), -jnp.inf, a_val + b_val)
            front = jnp.where(hit, nxt, front)
    return count, picked


def _top_k_no_ties(s, with_rank=True):
    rank = jnp.full(s.shape, float(PEER_TOPK), F32) if with_rank else None
    vals = []
    for r in range(PEER_TOPK):
        m = jnp.max(s, axis=0, keepdims=True)
        hit = s == m
        if with_rank:
            rank = jnp.where(hit, float(r), rank)
        s = jnp.where(hit, -jnp.inf, s)
        vals.append(m)
    members = jnp.sum(jnp.where(s == -jnp.inf, 1.0, 0.0), axis=0, keepdims=True)
    return vals, rank, members


def _merge_no_ties(v1, v2):
    sub = 8
    lo2 = jnp.concatenate(v2[:sub], axis=0)
    hi2 = jnp.concatenate(v2[sub:], axis=0)
    hi1 = jnp.concatenate(v1[sub:], axis=0)
    row = lax.broadcasted_iota(jnp.int32, lo2.shape, 0)
    cand, valid = [], []
    for a in range(sub):
        lim = min(PEER_TOPK // (a + 1), sub)
        ok = row < lim
        cand.append(jnp.where(ok, v1[a] + lo2, -jnp.inf))
        valid.append(ok)
    cand.append(v1[0] + hi2)
    cand.append(hi1 + v2[0])
    valid += [None, None]
    picked = []
    for r in range(PEER_TOPK):
        m = cand[0]
        for c in cand[1:]:
            m = jnp.maximum(m, c)
        m = jnp.max(m, axis=0, keepdims=True)
        picked.append(m)
        cand = [jnp.where(c == m, -jnp.inf, c) for c in cand]
    taken = []
    for c, ok in zip(cand, valid):
        t = c == -jnp.inf
        taken.append(jnp.where(t if ok is None else t & ok, 1.0, 0.0))
    counts = [jnp.sum(t, axis=0, keepdims=True) for t in taken[:sub]]
    counts[0] = counts[0] + jnp.sum(taken[sub], axis=0, keepdims=True)
    counts += [taken[sub + 1][a:a + 1] for a in range(sub)]
    total = counts[0]
    for cnt in counts[1:]:
        total = total + cnt
    return counts, picked, total


def _route_kernel(x_ref, wq_ref, keys_ref, rank2_ref, e2_ref, len1_ref, e1_ref, s_scr):
    tb = x_ref.shape[0]
    n_chunks = tb // LANES
    half = PEER_D_QUERY // 2
    xb = x_ref[...].astype(BF16)
    q_t = lax.dot_general(wq_ref[...], xb, NT_DIMS, preferred_element_type=F32)
    for h in range(PEER_HEADS):
        for p in range(2):
            r0 = h * PEER_D_QUERY + p * half
            s = jnp.dot(keys_ref[p], q_t[r0:r0 + half, :].astype(BF16), preferred_element_type=F32)
            for c in range(n_chunks):
                s_scr[2 * h + p, c] = s[:, c * LANES:(c + 1) * LANES]

    def emit(h, c, s1, s2, in_row, rank2, top1, top2, counts, picked):
        z = jnp.ones_like(picked[0])
        for r in range(1, PEER_TOPK):
            z = z + jnp.exp(picked[r] - picked[0])
        len1 = jnp.zeros(s1.shape, F32)
        for a in range(PEER_TOPK):
            len1 = jnp.where(in_row(a), counts[a], len1)
        rank2_ref[h, c] = rank2
        e2_ref[h, c] = jnp.exp(s2 - top2)
        len1_ref[h, c] = len1
        e1_ref[h, c] = jnp.exp(s1 - top1) * (0.5 / z)

    group = min(4, n_chunks)

    def body(idx, carry):
        h = idx // (n_chunks // group)
        c0 = (idx % (n_chunks // group)) * group
        flags = []
        for dc in range(group):
            c = c0 + dc
            s1 = s_scr[2 * h, c]
            s2 = s_scr[2 * h + 1, c]
            v1, _, n1 = _top_k_no_ties(s1, with_rank=False)
            v2, rank2, n2 = _top_k_no_ties(s2)
            counts, picked, n12 = _merge_no_ties(v1, v2)
            emit(h, c, s1, s2, lambda a: s1 == v1[a], rank2, v1[0], v2[0], counts, picked)
            k = float(PEER_TOPK)
            flags.append(jnp.abs(n1 - k) + jnp.abs(n2 - k) + jnp.abs(n12 - k))

        def redo(c):
            s1 = s_scr[2 * h, c]
            s2 = s_scr[2 * h + 1, c]
            sv1, rank1 = _top_k_ranked(s1)
            sv2, rank2 = _top_k_ranked(s2)
            count, picked = _merge_top_k(sv1, sv2)
            counts = [count[a:a + 1] for a in range(PEER_TOPK)]
            emit(h, c, s1, s2, lambda a: rank1 == float(a), rank2, sv1[0:1], sv2[0:1], counts, picked)

        @pl.when(jnp.max(sum(flags)) > 0.0)
        def _():
            for dc in range(group):
                pl.when(jnp.max(flags[dc]) > 0.0)(functools.partial(redo, c0 + dc))

        return carry

    lax.fori_loop(0, PEER_HEADS * n_chunks // group, body, 0)


def _route(x2d, w_q, sub_keys):
    t, d = x2d.shape
    tb = min(ROW_BLOCK, t)
    nc = tb // LANES
    table_shape = (PEER_HEADS, t // LANES, PEER_N_KEYS, LANES)
    table_spec = pl.BlockSpec((PEER_HEADS, nc, PEER_N_KEYS, LANES), lambda i: (0, i, 0, 0))
    return pl.pallas_call(
        _route_kernel,
        grid=(t // tb,),
        in_specs=[
            pl.BlockSpec((tb, d), lambda i: (i, 0)),
            _full((PEER_HEADS * PEER_D_QUERY, d)),
            _full((2, PEER_N_KEYS, PEER_D_QUERY // 2)),
        ],
        out_specs=[table_spec] * 4,
        out_shape=[jax.ShapeDtypeStruct(table_shape, F32)] * 4,
        scratch_shapes=[pltpu.VMEM((2 * PEER_HEADS, nc, PEER_N_KEYS, LANES), F32)],
        compiler_params=_params(1),
        name="peer_route",
    )(x2d, w_q.T.astype(BF16), sub_keys.astype(BF16))


def _pow2_scale(bound):
    bound = jnp.maximum(bound, FP8_MIN_BOUND)
    return jnp.exp2(jnp.floor(jnp.log2(FP8_TARGET / bound)))


def _abs_max(x):
    return jnp.max(jnp.max(jnp.abs(x), axis=0, keepdims=True), axis=1, keepdims=True)


def _expert_kernel(x_ref, u_ref, vt_ref, wscale_ref, rank2_ref, e2_ref, len1_ref, e1_ref, g_ref,
                   b_ref, o_ref, xt_scr, h_scr, a_scr, acc_scr, rank2_scr, e2_scr, inv_scr, *,
                   alpha, n_tiles):
    tb = x_ref.shape[0]
    te = u_ref.shape[0]
    slabs = te // PEER_N_KEYS
    n_chunks = tb // LANES
    pack = 16
    j = pl.program_id(1)

    @pl.when(j == 0)
    def _():
        x = x_ref[...]
        s_x = _pow2_scale(_abs_max(x))
        xt_scr[...] = (x * s_x).T.astype(F8)
        inv_scr[...] = jnp.broadcast_to(wscale_ref[0:1, :] / s_x, inv_scr.shape)
        acc_scr[...] = jnp.zeros(acc_scr.shape, F32)
        for h in range(PEER_HEADS):
            for c in range(n_chunks):
                rank2_scr[h, c] = rank2_ref[h, c].astype(BF16)
                e2_scr[h, c] = e2_ref[h, c].astype(BF16)

    h = jnp.dot(u_ref[...], xt_scr[...], preferred_element_type=F32)
    h_scr[...] = h
    inv_h = inv_scr[0:1, :]
    s_a = _pow2_scale(_abs_max(h) * inv_h * float(PEER_HEADS))

    def slab(s):
        i1 = j * slabs + s
        r0 = s * PEER_N_KEYS
        for c in range(n_chunks):
            lanes = slice(c * LANES, (c + 1) * LANES)
            length, e1 = [], []
            for h in range(PEER_HEADS):
                row = len1_ref[h, c, pl.ds(i1, 1), :]
                length.append(jnp.broadcast_to(row, (pack, LANES)).astype(BF16))
                row = e1_ref[h, c, pl.ds(i1, 1), :] * s_a
                e1.append(jnp.broadcast_to(row, (pack, LANES)).astype(BF16))
            outs = []
            for k in range(PEER_N_KEYS // pack):
                rows = slice(k * pack, (k + 1) * pack)
                gate = None
                for h in range(PEER_HEADS):
                    term = jnp.where(rank2_scr[h, c, rows, :] < length[h],
                                     e2_scr[h, c, rows, :], jnp.zeros((), BF16)) * e1[h]
                    gate = term if gate is None else gate + term
                pre = h_scr[r0 + k * pack:r0 + (k + 1) * pack, lanes] * inv_h
                act = pre * (1.0 + lax.erf(pre * (2.0 ** -0.5)))
                outs.append(act.astype(BF16) * gate)
                if k % 2 == 1:
                    both = jnp.concatenate(outs, axis=0)
                    outs = []
                    a_scr[r0 + (k - 1) * pack:r0 + (k + 1) * pack, lanes] = both.astype(F8)

    @pl.when(j >= 0)
    def _():
        for s in range(slabs):
            slab(s)

    y = jnp.dot(vt_ref[...], a_scr[...], preferred_element_type=F32)
    acc_scr[...] += y * (wscale_ref[1:2, 0:1] / s_a[:, 0:1])

    @pl.when(j == n_tiles - 1)
    def _():
        y = acc_scr[...].T
        o_ref[...] = _layer_norm(alpha * x_ref[...] + y, g_ref[...], b_ref[...])


def _experts(x2d, tables, u, v, g, b, alpha):
    t, d = x2d.shape
    n_experts = u.shape[0]
    tb = min(ROW_BLOCK, t)
    te = EXPERT_TILE
    n_tiles = n_experts // te
    nc = tb // LANES
    table_spec = pl.BlockSpec((PEER_HEADS, nc, PEER_N_KEYS, LANES), lambda i, j: (0, i, 0, 0))
    s_u = _pow2_scale(jnp.max(jnp.abs(u)))
    s_v = _pow2_scale(jnp.max(jnp.abs(v)))
    wscale = jnp.broadcast_to(jnp.stack([1.0 / s_u, 1.0 / s_v])[:, None], (2, LANES))
    return pl.pallas_call(
        functools.partial(_expert_kernel, alpha=alpha, n_tiles=n_tiles),
        grid=(t // tb, n_tiles),
        in_specs=[
            pl.BlockSpec((tb, d), lambda i, j: (i, 0)),
            pl.BlockSpec((te, d), lambda i, j: (j, 0)),
            pl.BlockSpec((d, te), lambda i, j: (0, j)),
            pl.BlockSpec((2, LANES), lambda i, j: (0, 0)),
            table_spec, table_spec, table_spec, table_spec,
            pl.BlockSpec((1, d), lambda i, j: (0, 0)),
            pl.BlockSpec((1, d), lambda i, j: (0, 0)),
        ],
        out_specs=pl.BlockSpec((tb, d), lambda i, j: (i, 0)),
        out_shape=jax.ShapeDtypeStruct((t, d), F32),
        scratch_shapes=[
            pltpu.VMEM((d, tb), F8),
            pltpu.VMEM((te, tb), F32),
            pltpu.VMEM((te, tb), F8),
            pltpu.VMEM((d, tb), F32),
            pltpu.VMEM((PEER_HEADS, nc, PEER_N_KEYS, LANES), BF16),
            pltpu.VMEM((PEER_HEADS, nc, PEER_N_KEYS, LANES), BF16),
            pltpu.VMEM((8, LANES), F32),
        ],
        compiler_params=_params(2),
        name="peer_experts",
    )(x2d, (u * s_u).astype(F8), (v.T * s_v).astype(F8), wscale, *tables,
      g.reshape(1, -1), b.reshape(1, -1))


def _peer_layer(x2d, w_q, sub_keys, u, v, g, b, alpha):
    tables = _route(x2d, w_q, sub_keys)
    return _experts(x2d, tables, u, v, g, b, alpha)


def kernel(x, positions, conv_w_in, conv_b_in, conv_w_dw, conv_b_dw, conv_ln_g, conv_ln_b,
           conv_w_out, conv_b_out, mla_w_in, mla_q_norm, mla_kv_norm, mla_w_uq, mla_w_uk,
           mla_w_uv, mla_w_out, peer_w_q, peer_sub_keys, peer_u, peer_v,
           ln1_g, ln1_b, ln2_g, ln2_b):
    batch, seq, d = x.shape
    depth = ln1_g.shape[0]
    alpha = (2 * depth) ** 0.25
    x2d = x.reshape(batch * seq, d)

    inv = 1.0 / (ROPE_THETA ** (jnp.arange(0, MLA_ROPE, 2, dtype=F32) / MLA_ROPE))
    ang = positions.astype(F32).reshape(batch * seq, 1) * inv
    ang = _pad_lanes(jnp.concatenate([ang, ang], axis=-1))

    for i in range(depth):
        j = i // 2
        if i % 2 == 0:
            x2d = _conv_layer(x2d, seq, conv_w_in[j], conv_b_in[j], conv_w_dw[j], conv_b_dw[j],
                              conv_ln_g[j], conv_ln_b[j], conv_w_out[j], conv_b_out[j],
                              ln1_g[i], ln1_b[i], alpha)
        else:
            x2d = _mla_layer(x2d, ang, batch, seq, mla_w_in[j], mla_q_norm[j], mla_kv_norm[j],
                             mla_w_uq[j], mla_w_uk[j], mla_w_uv[j], mla_w_out[j],
                             ln1_g[i], ln1_b[i], alpha)
        x2d = _peer_layer(x2d, peer_w_q[i], peer_sub_keys[i], peer_u[i], peer_v[i],
                          ln2_g[i], ln2_b[i], alpha)
    return x2d.reshape(batch, seq, d)
```

```python
import functools

import jax
import jax.numpy as jnp
from jax import lax
from jax.experimental import pallas as pl
from jax.experimental.pallas import tpu as pltpu

F32 = jnp.float32
BF16 = jnp.bfloat16
F8 = jnp.float8_e4m3fn

LANES = 128
VMEM_LIMIT = 58 * 1024 * 1024

D_MODEL = 1024
CHUNK = 64
CONV_WIDTH = 31
CONV_HALO = 32

MLA_HEADS = 8
MLA_NOPE = 128
MLA_ROPE = 64
MLA_V = 128
MLA_Q_RANK = 384
MLA_KV_RANK = 256
ROPE_THETA = 10000.0
QK_PAD = 256
ATTN_HEADS_PER_STEP = 2

PEER_HEADS = 8
PEER_N_KEYS = 128
PEER_D_QUERY = 256
PEER_TOPK = 16

LN_EPS = 1e-5
RMS_EPS = 1e-6

ROW_BLOCK = 512
EXPERT_TILE = 2048
FP8_TARGET = 224.0
FP8_MIN_BOUND = 1e-30

NT_DIMS = (((1,), (1,)), ((), ()))


def _params(n_axes, flags=None):
    return pltpu.CompilerParams(
        dimension_semantics=("arbitrary",) * n_axes, vmem_limit_bytes=VMEM_LIMIT, flags=flags)


def _layer_norm(x, g, b):
    mu = jnp.mean(x, axis=-1, keepdims=True)
    xc = x - mu
    var = jnp.mean(xc * xc, axis=-1, keepdims=True)
    return xc * lax.rsqrt(var + LN_EPS) * g + b


def _rms_norm(x, g):
    return x * lax.rsqrt(jnp.mean(x * x, axis=-1, keepdims=True) + RMS_EPS) * g


def _full(shape):
    return pl.BlockSpec(shape, lambda *_: (0,) * len(shape))


def _conv_layer_kernel(x_ref, w_in_ref, b_in_ref, w_dw_ref, b_dw_ref, lng_ref, lnb_ref,
                       w_out_ref, b_out_ref, l1g_ref, l1b_ref, o_ref, gbuf, cbuf, *,
                       blocks_per_seq, alpha):
    tm, d = x_ref.shape
    i = pl.program_id(0)

    @pl.when(i % blocks_per_seq == 0)
    def _():
        gbuf[0:CONV_HALO, :] = jnp.zeros((CONV_HALO, d), F32)

    x = x_ref[...]
    h = jnp.dot(x.astype(BF16), w_in_ref[...], preferred_element_type=F32) + b_in_ref[...]
    gbuf[CONV_HALO:CONV_HALO + tm, :] = h[:, :d] * jax.nn.sigmoid(h[:, d:])

    base = CONV_HALO - (CONV_WIDTH - 1)
    sublanes = 8
    taps_by_shift = {}
    for k in range(CONV_WIDTH):
        taps_by_shift.setdefault((base + k) % sublanes, []).append((k, (base + k) // sublanes))
    rc, lc = min(128, tm), 256
    for r0 in range(0, tm, rc):
        for l0 in range(0, d, lc):
            lanes = slice(l0, l0 + lc)
            acc = jnp.broadcast_to(b_dw_ref[:, lanes], (rc, lc))
            for shift, taps in taps_by_shift.items():
                span = sublanes * max(m for _, m in taps) + rc
                if shift:
                    n = span + sublanes
                    win = pltpu.roll(gbuf[r0:r0 + n, lanes], n - shift, axis=0)
                else:
                    win = gbuf[r0:r0 + span, lanes]
                for k, m in taps:
                    acc = acc + w_dw_ref[k:k + 1, lanes] * win[sublanes * m:sublanes * m + rc]
            cbuf[r0:r0 + rc, lanes] = acc
    gbuf[0:CONV_HALO, :] = gbuf[tm:tm + CONV_HALO, :]

    y = _layer_norm(cbuf[...], lng_ref[...], lnb_ref[...])
    y = y * jax.nn.sigmoid(y)
    mix = jnp.dot(y.astype(BF16), w_out_ref[...], preferred_element_type=F32) + b_out_ref[...]
    o_ref[...] = _layer_norm(alpha * x + mix, l1g_ref[...], l1b_ref[...])


def _conv_layer(x2d, seq, w_in, b_in, w_dw, b_dw, ln_g, ln_b, w_out, b_out, l1g, l1b, alpha):
    t, d = x2d.shape
    tm = min(ROW_BLOCK, seq)
    row = lambda v: v.reshape(1, -1)
    return pl.pallas_call(
        functools.partial(_conv_layer_kernel, blocks_per_seq=seq // tm, alpha=alpha),
        grid=(t // tm,),
        in_specs=[
            pl.BlockSpec((tm, d), lambda i: (i, 0)),
            _full((d, 2 * d)), _full((1, 2 * d)), _full((CONV_WIDTH, d)), _full((1, d)),
            _full((1, d)), _full((1, d)), _full((d, d)), _full((1, d)), _full((1, d)),
            _full((1, d)),
        ],
        out_specs=pl.BlockSpec((tm, d), lambda i: (i, 0)),
        out_shape=jax.ShapeDtypeStruct((t, d), F32),
        scratch_shapes=[pltpu.VMEM((CONV_HALO + tm, d), F32), pltpu.VMEM((tm, d), F32)],
        compiler_params=_params(1),
        name="conv_layer",
    )(x2d, w_in.astype(BF16), row(b_in), w_dw, row(b_dw), row(ln_g), row(ln_b),
      w_out.astype(BF16), row(b_out), row(l1g), row(l1b))


def _mla_proj_kernel(x_ref, ang_ref, w_in_ref, qn_ref, kvn_ref, w_uq_ref, w_uk_ref, w_uv_ref,
                     q_ref, k_ref, v_ref, *, scale):
    x = x_ref[...].astype(BF16)
    c = jnp.dot(x, w_in_ref[...], preferred_element_type=F32)
    cq = _rms_norm(c[:, :MLA_Q_RANK], qn_ref[...])
    kv_end = MLA_Q_RANK + MLA_KV_RANK
    ckv = _rms_norm(c[:, MLA_Q_RANK:kv_end], kvn_ref[...])
    ang = ang_ref[...]
    cos, sin = jnp.cos(ang), jnp.sin(ang)
    k_rope = c[:, kv_end:kv_end + LANES] * cos + c[:, kv_end + LANES:kv_end + 2 * LANES] * sin

    q = jnp.dot(cq.astype(BF16), w_uq_ref[...], preferred_element_type=F32)
    ckv_b = ckv.astype(BF16)
    kn = jnp.dot(ckv_b, w_uk_ref[...], preferred_element_type=F32)
    v_ref[...] = jnp.dot(ckv_b, w_uv_ref[...], preferred_element_type=F32).astype(BF16)
    k_rope_b = k_rope.astype(BF16)
    for h in range(MLA_HEADS):
        qh = q[:, h * 3 * LANES:(h + 1) * 3 * LANES]
        q_rope = qh[:, LANES:2 * LANES] * cos + qh[:, 2 * LANES:] * sin
        q_ref[:, h * QK_PAD:h * QK_PAD + LANES] = (qh[:, :LANES] * scale).astype(BF16)
        q_ref[:, h * QK_PAD + LANES:(h + 1) * QK_PAD] = (q_rope * scale).astype(BF16)
        k_ref[:, h * QK_PAD:h * QK_PAD + LANES] = kn[:, h * MLA_NOPE:(h + 1) * MLA_NOPE].astype(BF16)
        k_ref[:, h * QK_PAD + LANES:(h + 1) * QK_PAD] = k_rope_b


def _half_rotation(w):
    half = w.shape[-1] // 2
    return jnp.concatenate([-w[..., half:], w[..., :half]], axis=-1)


def _pad_lanes(w):
    return jnp.pad(w, [(0, 0)] * (w.ndim - 1) + [(0, LANES - w.shape[-1])])


def _mla_proj(x2d, ang, w_in, q_norm, kv_norm, w_uq, w_uk, w_uv):
    t, d = x2d.shape
    tm = min(ROW_BLOCK, t)
    kv_end = MLA_Q_RANK + MLA_KV_RANK
    w_rope = w_in[:, kv_end:]
    w_in_ext = jnp.concatenate(
        [w_in[:, :kv_end], _pad_lanes(w_rope), _pad_lanes(_half_rotation(w_rope))], axis=1).astype(BF16)
    wq = w_uq.reshape(MLA_Q_RANK, MLA_HEADS, MLA_NOPE + MLA_ROPE)
    wq_rope = wq[..., MLA_NOPE:]
    w_uq_ext = jnp.concatenate(
        [wq[..., :MLA_NOPE], _pad_lanes(wq_rope), _pad_lanes(_half_rotation(wq_rope))],
        axis=-1).reshape(MLA_Q_RANK, MLA_HEADS * 3 * LANES).astype(BF16)
    scale = (MLA_NOPE + MLA_ROPE) ** -0.5
    n_in = w_in_ext.shape[1]
    return pl.pallas_call(
        functools.partial(_mla_proj_kernel, scale=scale),
        grid=(t // tm,),
        in_specs=[
            pl.BlockSpec((tm, d), lambda i: (i, 0)),
            pl.BlockSpec((tm, LANES), lambda i: (i, 0)),
            _full((d, n_in)), _full((1, MLA_Q_RANK)), _full((1, MLA_KV_RANK)),
            _full((MLA_Q_RANK, MLA_HEADS * 3 * LANES)),
            _full((MLA_KV_RANK, MLA_HEADS * MLA_NOPE)), _full((MLA_KV_RANK, MLA_HEADS * MLA_V)),
        ],
        out_specs=[
            pl.BlockSpec((tm, MLA_HEADS * QK_PAD), lambda i: (i, 0)),
            pl.BlockSpec((tm, MLA_HEADS * QK_PAD), lambda i: (i, 0)),
            pl.BlockSpec((tm, MLA_HEADS * MLA_V), lambda i: (i, 0)),
        ],
        out_shape=[
            jax.ShapeDtypeStruct((t, MLA_HEADS * QK_PAD), BF16),
            jax.ShapeDtypeStruct((t, MLA_HEADS * QK_PAD), BF16),
            jax.ShapeDtypeStruct((t, MLA_HEADS * MLA_V), BF16),
        ],
        compiler_params=_params(1),
        name="mla_proj",
    )(x2d, ang, w_in_ext, q_norm.reshape(1, -1), kv_norm.reshape(1, -1), w_uq_ext,
      w_uk.astype(BF16), w_uv.astype(BF16))


def _attn_kernel(q_ref, k_ref, v_ref, o_ref):
    tq = q_ref.shape[0]
    qi = pl.program_id(2)
    heads = range(q_ref.shape[1] // QK_PAD)
    q = [q_ref[:, h * QK_PAD:(h + 1) * QK_PAD] for h in heads]

    def step(h, r0, carry, mask):
        m, l, acc = carry
        k = k_ref[pl.ds(r0, tq), h * QK_PAD:(h + 1) * QK_PAD]
        v = v_ref[pl.ds(r0, tq), h * MLA_V:(h + 1) * MLA_V]
        s = lax.dot_general(q[h], k, NT_DIMS, preferred_element_type=F32)
        if mask is not None:
            s = jnp.where(mask, s, jnp.finfo(F32).min)
        m_new = jnp.maximum(m, jnp.max(s, axis=-1, keepdims=True))
        corr = jnp.exp(m - m_new)
        p = jnp.exp(s - m_new)
        l = corr * l + jnp.sum(p, axis=-1, keepdims=True)
        acc = corr * acc + jnp.dot(p.astype(BF16), v, preferred_element_type=F32)
        return m_new, l, acc

    def body(kb, carries):
        r0 = pl.multiple_of(kb * tq, tq)
        return tuple(step(h, r0, carries[h], None) for h in heads)

    init = (jnp.full((tq, 1), -jnp.inf, F32), jnp.zeros((tq, 1), F32), jnp.zeros((tq, MLA_V), F32))
    carries = lax.fori_loop(0, qi, body, tuple(init for _ in heads))
    chunk_bits = CHUNK.bit_length() - 1
    q_chunk = lax.shift_right_logical(lax.broadcasted_iota(jnp.int32, (tq, tq), 0), chunk_bits)
    k_chunk = lax.shift_right_logical(lax.broadcasted_iota(jnp.int32, (tq, tq), 1), chunk_bits)
    r0 = pl.multiple_of(qi * tq, tq)
    for h in heads:
        _, l, acc = step(h, r0, carries[h], k_chunk <= q_chunk)
        o_ref[:, h * MLA_V:(h + 1) * MLA_V] = (acc / l).astype(BF16)


def _attention(q, k, v, batch, seq):
    t = q.shape[0]
    tq = min(ROW_BLOCK, seq)
    nq = seq // tq
    hp = ATTN_HEADS_PER_STEP
    return pl.pallas_call(
        _attn_kernel,
        grid=(batch, MLA_HEADS // hp, nq),
        in_specs=[
            pl.BlockSpec((tq, hp * QK_PAD), lambda b, h, i: (b * nq + i, h)),
            pl.BlockSpec((seq, hp * QK_PAD), lambda b, h, i: (b, h)),
            pl.BlockSpec((seq, hp * MLA_V), lambda b, h, i: (b, h)),
        ],
        out_specs=pl.BlockSpec((tq, hp * MLA_V), lambda b, h, i: (b * nq + i, h)),
        out_shape=jax.ShapeDtypeStruct((t, MLA_HEADS * MLA_V), BF16),
        compiler_params=_params(3),
        name="mla_attention",
    )(q, k, v)


def _out_proj_kernel(o_ref, x_ref, w_ref, g_ref, b_ref, y_ref, *, alpha):
    mix = jnp.dot(o_ref[...], w_ref[...], preferred_element_type=F32)
    y_ref[...] = _layer_norm(alpha * x_ref[...] + mix, g_ref[...], b_ref[...])


def _out_proj(o, x2d, w_out, g, b, alpha):
    t, d = x2d.shape
    tm = min(ROW_BLOCK, t)
    return pl.pallas_call(
        functools.partial(_out_proj_kernel, alpha=alpha),
        grid=(t // tm,),
        in_specs=[
            pl.BlockSpec((tm, o.shape[1]), lambda i: (i, 0)),
            pl.BlockSpec((tm, d), lambda i: (i, 0)),
            _full(w_out.shape), _full((1, d)), _full((1, d)),
        ],
        out_specs=pl.BlockSpec((tm, d), lambda i: (i, 0)),
        out_shape=jax.ShapeDtypeStruct((t, d), F32),
        compiler_params=_params(1),
        name="mla_out_proj",
    )(o, x2d, w_out.astype(BF16), g.reshape(1, -1), b.reshape(1, -1))


def _mla_layer(x2d, ang, batch, seq, w_in, q_norm, kv_norm, w_uq, w_uk, w_uv, w_out, l1g, l1b, alpha):
    q, k, v = _mla_proj(x2d, ang, w_in, q_norm, kv_norm, w_uq, w_uk, w_uv)
    o = _attention(q, k, v, batch, seq)
    return _out_proj(o, x2d, w_out, l1g, l1b, alpha)


def _top_k_ranked(s):
    n = s.shape[0]
    key_id = lax.broadcasted_iota(jnp.int32, s.shape, 0)
    rank = jnp.full(s.shape, float(PEER_TOPK), F32)
    vals = []
    for r in range(PEER_TOPK):
        m = jnp.max(s, axis=0, keepdims=True)
        first = jnp.min(jnp.where(s == m, key_id, n), axis=0, keepdims=True)
        hit = key_id == first
        rank = jnp.where(hit, float(r), rank)
        s = jnp.where(hit, -jnp.inf, s)
        vals.append(m)
    return jnp.concatenate(vals, axis=0), rank


def _merge_top_k(sv1, sv2):
    row_id = lax.broadcasted_iota(jnp.int32, sv1.shape, 0).astype(F32)
    count = jnp.zeros(sv1.shape, F32)
    front = sv1 + sv2[0:1]
    picked = []
    for r in range(PEER_TOPK):
        m = jnp.max(front, axis=0, keepdims=True)
        picked.append(m)
        first = jnp.min(jnp.where(front == m, row_id, float(PEER_TOPK)), axis=0, keepdims=True)
        hit = row_id == first
        count = jnp.where(hit, count + 1.0, count)
        if r + 1 < PEER_TOPK:
            new_len = jnp.sum(jnp.where(hit, count, 0.0), axis=0, keepdims=True)
            a_val = jnp.sum(jnp.where(hit, sv1, 0.0), axis=0, keepdims=True)
            b_val = jnp.sum(jnp.where(row_id == new_len, sv2, 0.0), axis=0, keepdims=True)
            nxt = jnp.where(new_len >= float(PEER_TOPK), -jnp.inf, a_val + b_val)
            front = jnp.where(hit, nxt, front)
    return count, picked


def _top_k_no_ties(s, with_rank=True):
    rank = jnp.full(s.shape, float(PEER_TOPK), F32) if with_rank else None
    vals = []
    for r in range(PEER_TOPK):
        m = jnp.max(s, axis=0, keepdims=True)
        hit = s == m
        if with_rank:
            rank = jnp.where(hit, float(r), rank)
        s = jnp.where(hit, -jnp.inf, s)
        vals.append(m)
    members = jnp.sum(jnp.where(s == -jnp.inf, 1.0, 0.0), axis=0, keepdims=True)
    return vals, rank, members


def _merge_no_ties(v1, v2):
    sub = 8
    lo2 = jnp.concatenate(v2[:sub], axis=0)
    hi2 = jnp.concatenate(v2[sub:], axis=0)
    hi1 = jnp.concatenate(v1[sub:], axis=0)
    row = lax.broadcasted_iota(jnp.int32, lo2.shape, 0)
    cand, valid = [], []
    for a in range(sub):
        lim = min(PEER_TOPK // (a + 1), sub)
        ok = row < lim
        cand.append(jnp.where(ok, v1[a] + lo2, -jnp.inf))
        valid.append(ok)
    cand.append(v1[0] + hi2)
    cand.append(hi1 + v2[0])
    valid += [None, None]
    picked = []
    for r in range(PEER_TOPK):
        m = cand[0]
        for c in cand[1:]:
            m = jnp.maximum(m, c)
        m = jnp.max(m, axis=0, keepdims=True)
        picked.append(m)
        cand = [jnp.where(c == m, -jnp.inf, c) for c in cand]
    taken = []
    for c, ok in zip(cand, valid):
        t = c == -jnp.inf
        taken.append(jnp.where(t if ok is None else t & ok, 1.0, 0.0))
    counts = [jnp.sum(t, axis=0, keepdims=True) for t in taken[:sub]]
    counts[0] = counts[0] + jnp.sum(taken[sub], axis=0, keepdims=True)
    counts += [taken[sub + 1][a:a + 1] for a in range(sub)]
    total = counts[0]
    for cnt in counts[1:]:
        total = total + cnt
    return counts, picked, total


def _route_kernel(x_ref, wq_ref, keys_ref, rank2_ref, e2_ref, len1_ref, e1_ref, s_scr):
    tb = x_ref.shape[0]
    n_chunks = tb // LANES
    half = PEER_D_QUERY // 2
    xb = x_ref[...].astype(BF16)
    q_t = lax.dot_general(wq_ref[...], xb, NT_DIMS, preferred_element_type=F32)
    for h in range(PEER_HEADS):
        for p in range(2):
            r0 = h * PEER_D_QUERY + p * half
            s = jnp.dot(keys_ref[p], q_t[r0:r0 + half, :].astype(BF16), preferred_element_type=F32)
            for c in range(n_chunks):
                s_scr[2 * h + p, c] = s[:, c * LANES:(c + 1) * LANES]

    def emit(h, c, s1, s2, in_row, rank2, top1, top2, counts, picked):
        z = jnp.ones_like(picked[0])
        for r in range(1, PEER_TOPK):
            z = z + jnp.exp(picked[r] - picked[0])
        len1 = jnp.zeros(s1.shape, F32)
        for a in range(PEER_TOPK):
            len1 = jnp.where(in_row(a), counts[a], len1)
        rank2_ref[h, c] = rank2
        e2_ref[h, c] = jnp.exp(s2 - top2)
        len1_ref[h, c] = len1
        e1_ref[h, c] = jnp.exp(s1 - top1) * (0.5 / z)

    group = min(4, n_chunks)

    def body(idx, carry):
        h = idx // (n_chunks // group)
        c0 = (idx % (n_chunks // group)) * group
        flags = []
        for dc in range(group):
            c = c0 + dc
            s1 = s_scr[2 * h, c]
            s2 = s_scr[2 * h + 1, c]
            v1, _, n1 = _top_k_no_ties(s1, with_rank=False)
            v2, rank2, n2 = _top_k_no_ties(s2)
            counts, picked, n12 = _merge_no_ties(v1, v2)
            emit(h, c, s1, s2, lambda a: s1 == v1[a], rank2, v1[0], v2[0], counts, picked)
            k = float(PEER_TOPK)
            flags.append(jnp.abs(n1 - k) + jnp.abs(n2 - k) + jnp.abs(n12 - k))

        def redo(c):
            s1 = s_scr[2 * h, c]
            s2 = s_scr[2 * h + 1, c]
            sv1, rank1 = _top_k_ranked(s1)
            sv2, rank2 = _top_k_ranked(s2)
            count, picked = _merge_top_k(sv1, sv2)
            counts = [count[a:a + 1] for a in range(PEER_TOPK)]
            emit(h, c, s1, s2, lambda a: rank1 == float(a), rank2, sv1[0:1], sv2[0:1], counts, picked)

        @pl.when(jnp.max(sum(flags)) > 0.0)
        def _():
            for dc in range(group):
                pl.when(jnp.max(flags[dc]) > 0.0)(functools.partial(redo, c0 + dc))

        return carry

    lax.fori_loop(0, PEER_HEADS * n_chunks // group, body, 0)


def _route(x2d, w_q, sub_keys):
    t, d = x2d.shape
    tb = min(ROW_BLOCK, t)
    nc = tb // LANES
    table_shape = (PEER_HEADS, t // LANES, PEER_N_KEYS, LANES)
    table_spec = pl.BlockSpec((PEER_HEADS, nc, PEER_N_KEYS, LANES), lambda i: (0, i, 0, 0))
    return pl.pallas_call(
        _route_kernel,
        grid=(t // tb,),
        in_specs=[
            pl.BlockSpec((tb, d), lambda i: (i, 0)),
            _full((PEER_HEADS * PEER_D_QUERY, d)),
            _full((2, PEER_N_KEYS, PEER_D_QUERY // 2)),
        ],
        out_specs=[table_spec] * 4,
        out_shape=[jax.ShapeDtypeStruct(table_shape, F32)] * 4,
        scratch_shapes=[pltpu.VMEM((2 * PEER_HEADS, nc, PEER_N_KEYS, LANES), F32)],
        compiler_params=_params(1),
        name="peer_route",
    )(x2d, w_q.T.astype(BF16), sub_keys.astype(BF16))


def _pow2_scale(bound):
    bound = jnp.maximum(bound, FP8_MIN_BOUND)
    return jnp.exp2(jnp.floor(jnp.log2(FP8_TARGET / bound)))


def _abs_max(x):
    return jnp.max(jnp.max(jnp.abs(x), axis=0, keepdims=True), axis=1, keepdims=True)


def _expert_kernel(x_ref, u_ref, v_ref, wscale_ref, rank2_ref, e2_ref, len1_ref, e1_ref, g_ref,
                   b_ref, o_ref, xt_scr, h_scr, a_scr, acc_scr, rank2_scr, e2_scr, inv_scr, *,
                   alpha, n_tiles):
    tb = x_ref.shape[0]
    te = u_ref.shape[0]
    slabs = te // PEER_N_KEYS
    n_chunks = tb // LANES
    pack = 16
    j = pl.program_id(1)

    @pl.when(j == 0)
    def _():
        x = x_ref[...]
        s_x = _pow2_scale(_abs_max(x))
        xt_scr[...] = (x * s_x).T.astype(F8)
        inv_scr[...] = jnp.broadcast_to(wscale_ref[0:1, :] / s_x, inv_scr.shape)
        acc_scr[...] = jnp.zeros(acc_scr.shape, F32)
        for h in range(PEER_HEADS):
            for c in range(n_chunks):
                rank2_scr[h, c] = rank2_ref[h, c].astype(BF16)
                e2_scr[h, c] = e2_ref[h, c].astype(BF16)

    h = jnp.dot(u_ref[...], xt_scr[...], preferred_element_type=F32)
    h_scr[...] = h
    inv_h = inv_scr[0:1, :]
    s_a = _pow2_scale(_abs_max(h) * inv_h * float(PEER_HEADS))
    gate_row = s_a * inv_h
    erf_row = inv_h * (2.0 ** -0.5)

    def slab(s):
        i1 = j * slabs + s
        r0 = s * PEER_N_KEYS
        for c in range(n_chunks):
            lanes = slice(c * LANES, (c + 1) * LANES)
            length, e1 = [], []
            for h in range(PEER_HEADS):
                row = len1_ref[h, c, pl.ds(i1, 1), :]
                length.append(jnp.broadcast_to(row, (pack, LANES)).astype(BF16))
                row = e1_ref[h, c, pl.ds(i1, 1), :] * gate_row
                e1.append(jnp.broadcast_to(row, (pack, LANES)).astype(BF16))
            outs = []
            for k in range(PEER_N_KEYS // pack):
                rows = slice(k * pack, (k + 1) * pack)
                gate = None
                for h in range(PEER_HEADS):
                    term = jnp.where(rank2_scr[h, c, rows, :] < length[h],
                                     e2_scr[h, c, rows, :], jnp.zeros((), BF16)) * e1[h]
                    gate = term if gate is None else gate + term
                raw = h_scr[r0 + k * pack:r0 + (k + 1) * pack, lanes]
                act = raw * (1.0 + lax.erf(raw * erf_row))
                outs.append(act.astype(BF16) * gate)
                if k % 2 == 1:
                    both = jnp.concatenate(outs, axis=0)
                    outs = []
                    a_scr[r0 + (k - 1) * pack:r0 + (k + 1) * pack, lanes] = both.astype(F8)

    @pl.when(j >= 0)
    def _():
        for s in range(slabs):
            slab(s)

    tn_dims = (((0,), (0,)), ((), ()))
    y = lax.dot_general(v_ref[...], a_scr[...], tn_dims, preferred_element_type=F32)
    acc_scr[...] += y * (wscale_ref[1:2, 0:1] / s_a[:, 0:1])

    @pl.when(j == n_tiles - 1)
    def _():
        y = acc_scr[...].T
        o_ref[...] = _layer_norm(alpha * x_ref[...] + y, g_ref[...], b_ref[...])


def _experts(x2d, tables, u, v, g, b, alpha):
    t, d = x2d.shape
    n_experts = u.shape[0]
    tb = min(ROW_BLOCK, t)
    te = EXPERT_TILE
    n_tiles = n_experts // te
    nc = tb // LANES
    table_spec = pl.BlockSpec((PEER_HEADS, nc, PEER_N_KEYS, LANES), lambda i, j: (0, i, 0, 0))
    s_u = _pow2_scale(jnp.max(jnp.abs(u)))
    s_v = _pow2_scale(jnp.max(jnp.abs(v)))
    wscale = jnp.broadcast_to(jnp.stack([1.0 / s_u, 1.0 / s_v])[:, None], (2, LANES))
    return pl.pallas_call(
        functools.partial(_expert_kernel, alpha=alpha, n_tiles=n_tiles),
        grid=(t // tb, n_tiles),
        in_specs=[
            pl.BlockSpec((tb, d), lambda i, j: (i, 0)),
            pl.BlockSpec((te, d), lambda i, j: (j, 0)),
            pl.BlockSpec((te, d), lambda i, j: (j, 0)),
            pl.BlockSpec((2, LANES), lambda i, j: (0, 0)),
            table_spec, table_spec, table_spec, table_spec,
            pl.BlockSpec((1, d), lambda i, j: (0, 0)),
            pl.BlockSpec((1, d), lambda i, j: (0, 0)),
        ],
        out_specs=pl.BlockSpec((tb, d), lambda i, j: (i, 0)),
        out_shape=jax.ShapeDtypeStruct((t, d), F32),
        scratch_shapes=[
            pltpu.VMEM((d, tb), F8),
            pltpu.VMEM((te, tb), F32),
            pltpu.VMEM((te, tb), F8),
            pltpu.VMEM((d, tb), F32),
            pltpu.VMEM((PEER_HEADS, nc, PEER_N_KEYS, LANES), BF16),
            pltpu.VMEM((PEER_HEADS, nc, PEER_N_KEYS, LANES), BF16),
            pltpu.VMEM((8, LANES), F32),
        ],
        compiler_params=_params(2),
        name="peer_experts",
    )(x2d, (u * s_u).astype(F8), (v * s_v).astype(F8), wscale, *tables,
      g.reshape(1, -1), b.reshape(1, -1))


def _peer_layer(x2d, w_q, sub_keys, u, v, g, b, alpha):
    tables = _route(x2d, w_q, sub_keys)
    return _experts(x2d, tables, u, v, g, b, alpha)


def kernel(x, positions, conv_w_in, conv_b_in, conv_w_dw, conv_b_dw, conv_ln_g, conv_ln_b,
           conv_w_out, conv_b_out, mla_w_in, mla_q_norm, mla_kv_norm, mla_w_uq, mla_w_uk,
           mla_w_uv, mla_w_out, peer_w_q, peer_sub_keys, peer_u, peer_v,
           ln1_g, ln1_b, ln2_g, ln2_b):
    batch, seq, d = x.shape
    depth = ln1_g.shape[0]
    alpha = (2 * depth) ** 0.25
    x2d = x.reshape(batch * seq, d)

    inv = 1.0 / (ROPE_THETA ** (jnp.arange(0, MLA_ROPE, 2, dtype=F32) / MLA_ROPE))
    ang = positions.astype(F32).reshape(batch * seq, 1) * inv
    ang = _pad_lanes(jnp.concatenate([ang, ang], axis=-1))

    for i in range(depth):
        j = i // 2
        if i % 2 == 0:
            x2d = _conv_layer(x2d, seq, conv_w_in[j], conv_b_in[j], conv_w_dw[j], conv_b_dw[j],
                              conv_ln_g[j], conv_ln_b[j], conv_w_out[j], conv_b_out[j],
                              ln1_g[i], ln1_b[i], alpha)
        else:
            x2d = _mla_layer(x2d, ang, batch, seq, mla_w_in[j], mla_q_norm[j], mla_kv_norm[j],
                             mla_w_uq[j], mla_w_uk[j], mla_w_uv[j], mla_w_out[j],
                             ln1_g[i], ln1_b[i], alpha)
        x2d = _peer_layer(x2d, peer_w_q[i], peer_sub_keys[i], peer_u[i], peer_v[i],
                          ln2_g[i], ln2_b[i], alpha)
    return x2d.reshape(batch, seq, d)
```

```python
import functools

import jax
import jax.numpy as jnp
from jax import lax
from jax.experimental import pallas as pl
from jax.experimental.pallas import tpu as pltpu

F32 = jnp.float32
BF16 = jnp.bfloat16
F8 = jnp.float8_e4m3fn

LANES = 128
VMEM_LIMIT = 58 * 1024 * 1024

D_MODEL = 1024
CHUNK = 64
CONV_WIDTH = 31
CONV_HALO = 32

MLA_HEADS = 8
MLA_NOPE = 128
MLA_ROPE = 64
MLA_V = 128
MLA_Q_RANK = 384
MLA_KV_RANK = 256
ROPE_THETA = 10000.0
QK_PAD = 256
ATTN_HEADS_PER_STEP = 2

PEER_HEADS = 8
PEER_N_KEYS = 128
PEER_D_QUERY = 256
PEER_TOPK = 16

LN_EPS = 1e-5
RMS_EPS = 1e-6

ROW_BLOCK = 512
EXPERT_TILE = 2048
FP8_TARGET = 224.0
FP8_MIN_BOUND = 1e-30

NT_DIMS = (((1,), (1,)), ((), ()))


def _params(n_axes, flags=None):
    return pltpu.CompilerParams(
        dimension_semantics=("arbitrary",) * n_axes, vmem_limit_bytes=VMEM_LIMIT, flags=flags)


def _layer_norm(x, g, b):
    mu = jnp.mean(x, axis=-1, keepdims=True)
    xc = x - mu
    var = jnp.mean(xc * xc, axis=-1, keepdims=True)
    return xc * lax.rsqrt(var + LN_EPS) * g + b


def _rms_norm(x, g):
    return x * lax.rsqrt(jnp.mean(x * x, axis=-1, keepdims=True) + RMS_EPS) * g


def _full(shape):
    return pl.BlockSpec(shape, lambda *_: (0,) * len(shape))


def _conv_layer_kernel(x_ref, w_in_ref, b_in_ref, w_dw_ref, b_dw_ref, lng_ref, lnb_ref,
                       w_out_ref, b_out_ref, l1g_ref, l1b_ref, o_ref, gbuf, cbuf, *,
                       blocks_per_seq, alpha):
    tm, d = x_ref.shape
    i = pl.program_id(0)

    @pl.when(i % blocks_per_seq == 0)
    def _():
        gbuf[0:CONV_HALO, :] = jnp.zeros((CONV_HALO, d), F32)

    x = x_ref[...]
    h = jnp.dot(x.astype(BF16), w_in_ref[...], preferred_element_type=F32) + b_in_ref[...]
    gbuf[CONV_HALO:CONV_HALO + tm, :] = h[:, :d] * jax.nn.sigmoid(h[:, d:])

    base = CONV_HALO - (CONV_WIDTH - 1)
    sublanes = 8
    taps_by_shift = {}
    for k in range(CONV_WIDTH):
        taps_by_shift.setdefault((base + k) % sublanes, []).append((k, (base + k) // sublanes))
    rc, lc = min(128, tm), 256
    for r0 in range(0, tm, rc):
        for l0 in range(0, d, lc):
            lanes = slice(l0, l0 + lc)
            acc = jnp.broadcast_to(b_dw_ref[:, lanes], (rc, lc))
            for shift, taps in taps_by_shift.items():
                span = sublanes * max(m for _, m in taps) + rc
                if shift:
                    n = span + sublanes
                    win = pltpu.roll(gbuf[r0:r0 + n, lanes], n - shift, axis=0)
                else:
                    win = gbuf[r0:r0 + span, lanes]
                for k, m in taps:
                    acc = acc + w_dw_ref[k:k + 1, lanes] * win[sublanes * m:sublanes * m + rc]
            cbuf[r0:r0 + rc, lanes] = acc
    gbuf[0:CONV_HALO, :] = gbuf[tm:tm + CONV_HALO, :]

    y = _layer_norm(cbuf[...], lng_ref[...], lnb_ref[...])
    y = y * jax.nn.sigmoid(y)
    mix = jnp.dot(y.astype(BF16), w_out_ref[...], preferred_element_type=F32) + b_out_ref[...]
    o_ref[...] = _layer_norm(alpha * x + mix, l1g_ref[...], l1b_ref[...])


def _conv_layer(x2d, seq, w_in, b_in, w_dw, b_dw, ln_g, ln_b, w_out, b_out, l1g, l1b, alpha):
    t, d = x2d.shape
    tm = min(ROW_BLOCK, seq)
    row = lambda v: v.reshape(1, -1)
    return pl.pallas_call(
        functools.partial(_conv_layer_kernel, blocks_per_seq=seq // tm, alpha=alpha),
        grid=(t // tm,),
        in_specs=[
            pl.BlockSpec((tm, d), lambda i: (i, 0)),
            _full((d, 2 * d)), _full((1, 2 * d)), _full((CONV_WIDTH, d)), _full((1, d)),
            _full((1, d)), _full((1, d)), _full((d, d)), _full((1, d)), _full((1, d)),
            _full((1, d)),
        ],
        out_specs=pl.BlockSpec((tm, d), lambda i: (i, 0)),
        out_shape=jax.ShapeDtypeStruct((t, d), F32),
        scratch_shapes=[pltpu.VMEM((CONV_HALO + tm, d), F32), pltpu.VMEM((tm, d), F32)],
        compiler_params=_params(1),
        name="conv_layer",
    )(x2d, w_in.astype(BF16), row(b_in), w_dw, row(b_dw), row(ln_g), row(ln_b),
      w_out.astype(BF16), row(b_out), row(l1g), row(l1b))


def _mla_proj_kernel(x_ref, ang_ref, w_in_ref, qn_ref, kvn_ref, w_uq_ref, w_uk_ref, w_uv_ref,
                     q_ref, k_ref, v_ref, *, scale):
    x = x_ref[...].astype(BF16)
    c = jnp.dot(x, w_in_ref[...], preferred_element_type=F32)
    cq = _rms_norm(c[:, :MLA_Q_RANK], qn_ref[...])
    kv_end = MLA_Q_RANK + MLA_KV_RANK
    ckv = _rms_norm(c[:, MLA_Q_RANK:kv_end], kvn_ref[...])
    ang = ang_ref[...]
    cos, sin = jnp.cos(ang), jnp.sin(ang)
    k_rope = c[:, kv_end:kv_end + LANES] * cos + c[:, kv_end + LANES:kv_end + 2 * LANES] * sin

    q = jnp.dot(cq.astype(BF16), w_uq_ref[...], preferred_element_type=F32)
    ckv_b = ckv.astype(BF16)
    kn = jnp.dot(ckv_b, w_uk_ref[...], preferred_element_type=F32)
    v_ref[...] = jnp.dot(ckv_b, w_uv_ref[...], preferred_element_type=F32).astype(BF16)
    k_rope_b = k_rope.astype(BF16)
    for h in range(MLA_HEADS):
        qh = q[:, h * 3 * LANES:(h + 1) * 3 * LANES]
        q_rope = qh[:, LANES:2 * LANES] * cos + qh[:, 2 * LANES:] * sin
        q_ref[:, h * QK_PAD:h * QK_PAD + LANES] = (qh[:, :LANES] * scale).astype(BF16)
        q_ref[:, h * QK_PAD + LANES:(h + 1) * QK_PAD] = (q_rope * scale).astype(BF16)
        k_ref[:, h * QK_PAD:h * QK_PAD + LANES] = kn[:, h * MLA_NOPE:(h + 1) * MLA_NOPE].astype(BF16)
        k_ref[:, h * QK_PAD + LANES:(h + 1) * QK_PAD] = k_rope_b


def _half_rotation(w):
    half = w.shape[-1] // 2
    return jnp.concatenate([-w[..., half:], w[..., :half]], axis=-1)


def _pad_lanes(w):
    return jnp.pad(w, [(0, 0)] * (w.ndim - 1) + [(0, LANES - w.shape[-1])])


def _mla_proj(x2d, ang, w_in, q_norm, kv_norm, w_uq, w_uk, w_uv):
    t, d = x2d.shape
    tm = min(ROW_BLOCK, t)
    kv_end = MLA_Q_RANK + MLA_KV_RANK
    w_rope = w_in[:, kv_end:]
    w_in_ext = jnp.concatenate(
        [w_in[:, :kv_end], _pad_lanes(w_rope), _pad_lanes(_half_rotation(w_rope))], axis=1).astype(BF16)
    wq = w_uq.reshape(MLA_Q_RANK, MLA_HEADS, MLA_NOPE + MLA_ROPE)
    wq_rope = wq[..., MLA_NOPE:]
    w_uq_ext = jnp.concatenate(
        [wq[..., :MLA_NOPE], _pad_lanes(wq_rope), _pad_lanes(_half_rotation(wq_rope))],
        axis=-1).reshape(MLA_Q_RANK, MLA_HEADS * 3 * LANES).astype(BF16)
    scale = (MLA_NOPE + MLA_ROPE) ** -0.5
    n_in = w_in_ext.shape[1]
    return pl.pallas_call(
        functools.partial(_mla_proj_kernel, scale=scale),
        grid=(t // tm,),
        in_specs=[
            pl.BlockSpec((tm, d), lambda i: (i, 0)),
            pl.BlockSpec((tm, LANES), lambda i: (i, 0)),
            _full((d, n_in)), _full((1, MLA_Q_RANK)), _full((1, MLA_KV_RANK)),
            _full((MLA_Q_RANK, MLA_HEADS * 3 * LANES)),
            _full((MLA_KV_RANK, MLA_HEADS * MLA_NOPE)), _full((MLA_KV_RANK, MLA_HEADS * MLA_V)),
        ],
        out_specs=[
            pl.BlockSpec((tm, MLA_HEADS * QK_PAD), lambda i: (i, 0)),
            pl.BlockSpec((tm, MLA_HEADS * QK_PAD), lambda i: (i, 0)),
            pl.BlockSpec((tm, MLA_HEADS * MLA_V), lambda i: (i, 0)),
        ],
        out_shape=[
            jax.ShapeDtypeStruct((t, MLA_HEADS * QK_PAD), BF16),
            jax.ShapeDtypeStruct((t, MLA_HEADS * QK_PAD), BF16),
            jax.ShapeDtypeStruct((t, MLA_HEADS * MLA_V), BF16),
        ],
        compiler_params=_params(1),
        name="mla_proj",
    )(x2d, ang, w_in_ext, q_norm.reshape(1, -1), kv_norm.reshape(1, -1), w_uq_ext,
      w_uk.astype(BF16), w_uv.astype(BF16))


def _attn_kernel(q_ref, k_ref, v_ref, o_ref):
    tq = q_ref.shape[0]
    qi = pl.program_id(2)
    heads = range(q_ref.shape[1] // QK_PAD)
    q = [q_ref[:, h * QK_PAD:(h + 1) * QK_PAD] for h in heads]

    def step(h, r0, carry, mask):
        m, l, acc = carry
        k = k_ref[pl.ds(r0, tq), h * QK_PAD:(h + 1) * QK_PAD]
        v = v_ref[pl.ds(r0, tq), h * MLA_V:(h + 1) * MLA_V]
        s = lax.dot_general(q[h], k, NT_DIMS, preferred_element_type=F32)
        if mask is not None:
            s = jnp.where(mask, s, jnp.finfo(F32).min)
        m_new = jnp.maximum(m, jnp.max(s, axis=-1, keepdims=True))
        corr = jnp.exp(m - m_new)
        p = jnp.exp(s - m_new)
        l = corr * l + jnp.sum(p, axis=-1, keepdims=True)
        acc = corr * acc + jnp.dot(p.astype(BF16), v, preferred_element_type=F32)
        return m_new, l, acc

    def body(kb, carries):
        r0 = pl.multiple_of(kb * tq, tq)
        return tuple(step(h, r0, carries[h], None) for h in heads)

    init = (jnp.full((tq, 1), -jnp.inf, F32), jnp.zeros((tq, 1), F32), jnp.zeros((tq, MLA_V), F32))
    carries = lax.fori_loop(0, qi, body, tuple(init for _ in heads))
    chunk_bits = CHUNK.bit_length() - 1
    q_chunk = lax.shift_right_logical(lax.broadcasted_iota(jnp.int32, (tq, tq), 0), chunk_bits)
    k_chunk = lax.shift_right_logical(lax.broadcasted_iota(jnp.int32, (tq, tq), 1), chunk_bits)
    r0 = pl.multiple_of(qi * tq, tq)
    for h in heads:
        _, l, acc = step(h, r0, carries[h], k_chunk <= q_chunk)
        o_ref[:, h * MLA_V:(h + 1) * MLA_V] = (acc / l).astype(BF16)


def _attention(q, k, v, batch, seq):
    t = q.shape[0]
    tq = min(ROW_BLOCK, seq)
    nq = seq // tq
    hp = ATTN_HEADS_PER_STEP
    return pl.pallas_call(
        _attn_kernel,
        grid=(batch, MLA_HEADS // hp, nq),
        in_specs=[
            pl.BlockSpec((tq, hp * QK_PAD), lambda b, h, i: (b * nq + i, h)),
            pl.BlockSpec((seq, hp * QK_PAD), lambda b, h, i: (b, h)),
            pl.BlockSpec((seq, hp * MLA_V), lambda b, h, i: (b, h)),
        ],
        out_specs=pl.BlockSpec((tq, hp * MLA_V), lambda b, h, i: (b * nq + i, h)),
        out_shape=jax.ShapeDtypeStruct((t, MLA_HEADS * MLA_V), BF16),
        compiler_params=_params(3),
        name="mla_attention",
    )(q, k, v)


def _out_proj_kernel(o_ref, x_ref, w_ref, g_ref, b_ref, y_ref, *, alpha):
    mix = jnp.dot(o_ref[...], w_ref[...], preferred_element_type=F32)
    y_ref[...] = _layer_norm(alpha * x_ref[...] + mix, g_ref[...], b_ref[...])


def _out_proj(o, x2d, w_out, g, b, alpha):
    t, d = x2d.shape
    tm = min(ROW_BLOCK, t)
    return pl.pallas_call(
        functools.partial(_out_proj_kernel, alpha=alpha),
        grid=(t // tm,),
        in_specs=[
            pl.BlockSpec((tm, o.shape[1]), lambda i: (i, 0)),
            pl.BlockSpec((tm, d), lambda i: (i, 0)),
            _full(w_out.shape), _full((1, d)), _full((1, d)),
        ],
        out_specs=pl.BlockSpec((tm, d), lambda i: (i, 0)),
        out_shape=jax.ShapeDtypeStruct((t, d), F32),
        compiler_params=_params(1),
        name="mla_out_proj",
    )(o, x2d, w_out.astype(BF16), g.reshape(1, -1), b.reshape(1, -1))


def _mla_layer(x2d, ang, batch, seq, w_in, q_norm, kv_norm, w_uq, w_uk, w_uv, w_out, l1g, l1b, alpha):
    q, k, v = _mla_proj(x2d, ang, w_in, q_norm, kv_norm, w_uq, w_uk, w_uv)
    o = _attention(q, k, v, batch, seq)
    return _out_proj(o, x2d, w_out, l1g, l1b, alpha)


def _top_k_ranked(s):
    n = s.shape[0]
    key_id = lax.broadcasted_iota(jnp.int32, s.shape, 0)
    rank = jnp.full(s.shape, float(PEER_TOPK), F32)
    vals = []
    for r in range(PEER_TOPK):
        m = jnp.max(s, axis=0, keepdims=True)
        first = jnp.min(jnp.where(s == m, key_id, n), axis=0, keepdims=True)
        hit = key_id == first
        rank = jnp.where(hit, float(r), rank)
        s = jnp.where(hit, -jnp.inf, s)
        vals.append(m)
    return jnp.concatenate(vals, axis=0), rank


def _merge_top_k(sv1, sv2):
    row_id = lax.broadcasted_iota(jnp.int32, sv1.shape, 0).astype(F32)
    count = jnp.zeros(sv1.shape, F32)
    front = sv1 + sv2[0:1]
    picked = []
    for r in range(PEER_TOPK):
        m = jnp.max(front, axis=0, keepdims=True)
        picked.append(m)
        first = jnp.min(jnp.where(front == m, row_id, float(PEER_TOPK)), axis=0, keepdims=True)
        hit = row_id == first
        count = jnp.where(hit, count + 1.0, count)
        if r + 1 < PEER_TOPK:
            new_len = jnp.sum(jnp.where(hit, count, 0.0), axis=0, keepdims=True)
            a_val = jnp.sum(jnp.where(hit, sv1, 0.0), axis=0, keepdims=True)
            b_val = jnp.sum(jnp.where(row_id == new_len, sv2, 0.0), axis=0, keepdims=True)
            nxt = jnp.where(new_len >= float(PEER_TOPK), -jnp.inf, a_val + b_val)
            front = jnp.where(hit, nxt, front)
    return count, picked


def _top_k_no_ties(s, with_rank=True):
    rank = jnp.full(s.shape, float(PEER_TOPK), F32) if with_rank else None
    vals = []
    for r in range(PEER_TOPK):
        m = jnp.max(s, axis=0, keepdims=True)
        hit = s == m
        if with_rank:
            rank = jnp.where(hit, float(r), rank)
        s = jnp.where(hit, -jnp.inf, s)
        vals.append(m)
    members = jnp.sum(jnp.where(s == -jnp.inf, 1.0, 0.0), axis=0, keepdims=True)
    return vals, rank, members


def _merge_no_ties(v1, v2):
    sub = 8
    lo2 = jnp.concatenate(v2[:sub], axis=0)
    hi2 = jnp.concatenate(v2[sub:], axis=0)
    hi1 = jnp.concatenate(v1[sub:], axis=0)
    row = lax.broadcasted_iota(jnp.int32, lo2.shape, 0)
    cand, valid = [], []
    for a in range(sub):
        lim = min(PEER_TOPK // (a + 1), sub)
        ok = row < lim
        cand.append(jnp.where(ok, v1[a] + lo2, -jnp.inf))
        valid.append(ok)
    cand.append(v1[0] + hi2)
    cand.append(hi1 + v2[0])
    valid += [None, None]
    picked = []
    for r in range(PEER_TOPK):
        m = cand[0]
        for c in cand[1:]:
            m = jnp.maximum(m, c)
        m = jnp.max(m, axis=0, keepdims=True)
        picked.append(m)
        cand = [jnp.where(c == m, -jnp.inf, c) for c in cand]
    taken = []
    for c, ok in zip(cand, valid):
        t = c == -jnp.inf
        taken.append(jnp.where(t if ok is None else t & ok, 1.0, 0.0))
    counts = [jnp.sum(t, axis=0, keepdims=True) for t in taken[:sub]]
    counts[0] = counts[0] + jnp.sum(taken[sub], axis=0, keepdims=True)
    counts += [taken[sub + 1][a:a + 1] for a in range(sub)]
    total = counts[0]
    for cnt in counts[1:]:
        total = total + cnt
    return counts, picked, total


def _fold_keys_kernel(wq_ref, keys_ref, o_ref):
    half = PEER_D_QUERY // 2
    for p in range(2):
        o_ref[p * PEER_N_KEYS:(p + 1) * PEER_N_KEYS, :] = jnp.dot(
            keys_ref[p], wq_ref[p * half:(p + 1) * half, :], preferred_element_type=F32).astype(BF16)


def _fold_keys(w_q, sub_keys):
    d = w_q.shape[0]
    rows = 2 * PEER_N_KEYS
    return pl.pallas_call(
        _fold_keys_kernel,
        grid=(PEER_HEADS,),
        in_specs=[
            pl.BlockSpec((PEER_D_QUERY, d), lambda h: (h, 0)),
            _full((2, PEER_N_KEYS, PEER_D_QUERY // 2)),
        ],
        out_specs=pl.BlockSpec((rows, d), lambda h: (h, 0)),
        out_shape=jax.ShapeDtypeStruct((PEER_HEADS * rows, d), BF16),
        compiler_params=_params(1),
        name="peer_fold_keys",
    )(w_q.T.astype(BF16), sub_keys.astype(BF16))


def _route_kernel(x_ref, ws_ref, rank2_ref, e2_ref, len1_ref, e1_ref, s_scr):
    tb = x_ref.shape[0]
    n_chunks = tb // LANES
    xb = x_ref[...].astype(BF16)
    s = lax.dot_general(ws_ref[...], xb, NT_DIMS, preferred_element_type=F32)
    for hp in range(2 * PEER_HEADS):
        for c in range(n_chunks):
            s_scr[hp, c] = s[hp * PEER_N_KEYS:(hp + 1) * PEER_N_KEYS, c * LANES:(c + 1) * LANES]

    def emit(h, c, s1, s2, in_row, rank2, top1, top2, counts, picked):
        z = jnp.ones_like(picked[0])
        for r in range(1, PEER_TOPK):
            z = z + jnp.exp(picked[r] - picked[0])
        len1 = jnp.zeros(s1.shape, F32)
        for a in range(PEER_TOPK):
            len1 = jnp.where(in_row(a), counts[a], len1)
        rank2_ref[h, c] = rank2
        e2_ref[h, c] = jnp.exp(s2 - top2)
        len1_ref[h, c] = len1
        e1_ref[h, c] = jnp.exp(s1 - top1) * (0.5 / z)

    group = min(4, n_chunks)

    def body(idx, carry):
        h = idx // (n_chunks // group)
        c0 = (idx % (n_chunks // group)) * group
        flags = []
        for dc in range(group):
            c = c0 + dc
            s1 = s_scr[2 * h, c]
            s2 = s_scr[2 * h + 1, c]
            v1, _, n1 = _top_k_no_ties(s1, with_rank=False)
            v2, rank2, n2 = _top_k_no_ties(s2)
            counts, picked, n12 = _merge_no_ties(v1, v2)
            emit(h, c, s1, s2, lambda a: s1 == v1[a], rank2, v1[0], v2[0], counts, picked)
            k = float(PEER_TOPK)
            flags.append(jnp.abs(n1 - k) + jnp.abs(n2 - k) + jnp.abs(n12 - k))

        def redo(c):
            s1 = s_scr[2 * h, c]
            s2 = s_scr[2 * h + 1, c]
            sv1, rank1 = _top_k_ranked(s1)
            sv2, rank2 = _top_k_ranked(s2)
            count, picked = _merge_top_k(sv1, sv2)
            counts = [count[a:a + 1] for a in range(PEER_TOPK)]
            emit(h, c, s1, s2, lambda a: rank1 == float(a), rank2, sv1[0:1], sv2[0:1], counts, picked)

        @pl.when(jnp.max(sum(flags)) > 0.0)
        def _():
            for dc in range(group):
                pl.when(jnp.max(flags[dc]) > 0.0)(functools.partial(redo, c0 + dc))

        return carry

    lax.fori_loop(0, PEER_HEADS * n_chunks // group, body, 0)


def _route(x2d, w_q, sub_keys):
    t, d = x2d.shape
    tb = min(ROW_BLOCK, t)
    nc = tb // LANES
    table_shape = (PEER_HEADS, t // LANES, PEER_N_KEYS, LANES)
    table_spec = pl.BlockSpec((PEER_HEADS, nc, PEER_N_KEYS, LANES), lambda i: (0, i, 0, 0))
    return pl.pallas_call(
        _route_kernel,
        grid=(t // tb,),
        in_specs=[
            pl.BlockSpec((tb, d), lambda i: (i, 0)),
            _full((PEER_HEADS * 2 * PEER_N_KEYS, d)),
        ],
        out_specs=[table_spec] * 4,
        out_shape=[jax.ShapeDtypeStruct(table_shape, F32)] * 4,
        scratch_shapes=[pltpu.VMEM((2 * PEER_HEADS, nc, PEER_N_KEYS, LANES), F32)],
        compiler_params=_params(1),
        name="peer_route",
    )(x2d, _fold_keys(w_q, sub_keys))


def _pow2_scale(bound):
    bound = jnp.maximum(bound, FP8_MIN_BOUND)
    return jnp.exp2(jnp.floor(jnp.log2(FP8_TARGET / bound)))


def _abs_max(x):
    return jnp.max(jnp.max(jnp.abs(x), axis=0, keepdims=True), axis=1, keepdims=True)


def _expert_kernel(x_ref, u_ref, v_ref, wscale_ref, rank2_ref, e2_ref, len1_ref, e1_ref, g_ref,
                   b_ref, o_ref, xt_scr, h_scr, a_scr, acc_scr, rank2_scr, e2_scr, inv_scr, *,
                   alpha, n_tiles):
    tb = x_ref.shape[0]
    te = u_ref.shape[0]
    slabs = te // PEER_N_KEYS
    n_chunks = tb // LANES
    pack = 16
    j = pl.program_id(1)

    @pl.when(j == 0)
    def _():
        x = x_ref[...]
        s_x = _pow2_scale(_abs_max(x))
        xt_scr[...] = (x * s_x).T.astype(F8)
        inv_scr[...] = jnp.broadcast_to(wscale_ref[0:1, :] / s_x, inv_scr.shape)
        acc_scr[...] = jnp.zeros(acc_scr.shape, F32)
        for h in range(PEER_HEADS):
            for c in range(n_chunks):
                rank2_scr[h, c] = rank2_ref[h, c].astype(BF16)
                e2_scr[h, c] = e2_ref[h, c].astype(BF16)

    h = jnp.dot(u_ref[...], xt_scr[...], preferred_element_type=F32)
    h_scr[...] = h
    inv_h = inv_scr[0:1, :]
    s_a = _pow2_scale(_abs_max(h) * inv_h * float(PEER_HEADS))
    gate_row = s_a * inv_h
    erf_row = inv_h * (2.0 ** -0.5)

    def slab(s):
        i1 = j * slabs + s
        r0 = s * PEER_N_KEYS
        for c in range(n_chunks):
            lanes = slice(c * LANES, (c + 1) * LANES)
            length, e1 = [], []
            for h in range(PEER_HEADS):
                row = len1_ref[h, c, pl.ds(i1, 1), :]
                length.append(jnp.broadcast_to(row, (pack, LANES)).astype(BF16))
                row = e1_ref[h, c, pl.ds(i1, 1), :] * gate_row
                e1.append(jnp.broadcast_to(row, (pack, LANES)).astype(BF16))
            outs = []
            for k in range(PEER_N_KEYS // pack):
                rows = slice(k * pack, (k + 1) * pack)
                gate = None
                for h in range(PEER_HEADS):
                    term = jnp.where(rank2_scr[h, c, rows, :] < length[h],
                                     e2_scr[h, c, rows, :], jnp.zeros((), BF16)) * e1[h]
                    gate = term if gate is None else gate + term
                raw = h_scr[r0 + k * pack:r0 + (k + 1) * pack, lanes]
                act = raw * (1.0 + lax.erf(raw * erf_row))
                outs.append(act.astype(BF16) * gate)
                if k % 2 == 1:
                    both = jnp.concatenate(outs, axis=0)
                    outs = []
                    a_scr[r0 + (k - 1) * pack:r0 + (k + 1) * pack, lanes] = both.astype(F8)

    @pl.when(j >= 0)
    def _():
        for s in range(slabs):
            slab(s)

    tn_dims = (((0,), (0,)), ((), ()))
    y = lax.dot_general(v_ref[...], a_scr[...], tn_dims, preferred_element_type=F32)
    acc_scr[...] += y * (wscale_ref[1:2, 0:1] / s_a[:, 0:1])

    @pl.when(j == n_tiles - 1)
    def _():
        y = acc_scr[...].T
        o_ref[...] = _layer_norm(alpha * x_ref[...] + y, g_ref[...], b_ref[...])


def _experts(x2d, tables, u, v, g, b, alpha):
    t, d = x2d.shape
    n_experts = u.shape[0]
    tb = min(ROW_BLOCK, t)
    te = EXPERT_TILE
    n_tiles = n_experts // te
    nc = tb // LANES
    table_spec = pl.BlockSpec((PEER_HEADS, nc, PEER_N_KEYS, LANES), lambda i, j: (0, i, 0, 0))
    s_u = _pow2_scale(jnp.max(jnp.abs(u)))
    s_v = _pow2_scale(jnp.max(jnp.abs(v)))
    wscale = jnp.broadcast_to(jnp.stack([1.0 / s_u, 1.0 / s_v])[:, None], (2, LANES))
    return pl.pallas_call(
        functools.partial(_expert_kernel, alpha=alpha, n_tiles=n_tiles),
        grid=(t // tb, n_tiles),
        in_specs=[
            pl.BlockSpec((tb, d), lambda i, j: (i, 0)),
            pl.BlockSpec((te, d), lambda i, j: (j, 0)),
            pl.BlockSpec((te, d), lambda i, j: (j, 0)),
            pl.BlockSpec((2, LANES), lambda i, j: (0, 0)),
            table_spec, table_spec, table_spec, table_spec,
            pl.BlockSpec((1, d), lambda i, j: (0, 0)),
            pl.BlockSpec((1, d), lambda i, j: (0, 0)),
        ],
        out_specs=pl.BlockSpec((tb, d), lambda i, j: (i, 0)),
        out_shape=jax.ShapeDtypeStruct((t, d), F32),
        scratch_shapes=[
            pltpu.VMEM((d, tb), F8),
            pltpu.VMEM((te, tb), F32),
            pltpu.VMEM((te, tb), F8),
            pltpu.VMEM((d, tb), F32),
            pltpu.VMEM((PEER_HEADS, nc, PEER_N_KEYS, LANES), BF16),
            pltpu.VMEM((PEER_HEADS, nc, PEER_N_KEYS, LANES), BF16),
            pltpu.VMEM((8, LANES), F32),
        ],
        compiler_params=_params(2),
        name="peer_experts",
    )(x2d, (u * s_u).astype(F8), (v * s_v).astype(F8), wscale, *tables,
      g.reshape(1, -1), b.reshape(1, -1))


def _peer_layer(x2d, w_q, sub_keys, u, v, g, b, alpha):
    tables = _route(x2d, w_q, sub_keys)
    return _experts(x2d, tables, u, v, g, b, alpha)


def kernel(x, positions, conv_w_in, conv_b_in, conv_w_dw, conv_b_dw, conv_ln_g, conv_ln_b,
           conv_w_out, conv_b_out, mla_w_in, mla_q_norm, mla_kv_norm, mla_w_uq, mla_w_uk,
           mla_w_uv, mla_w_out, peer_w_q, peer_sub_keys, peer_u, peer_v,
           ln1_g, ln1_b, ln2_g, ln2_b):
    batch, seq, d = x.shape
    depth = ln1_g.shape[0]
    alpha = (2 * depth) ** 0.25
    x2d = x.reshape(batch * seq, d)

    inv = 1.0 / (ROPE_THETA ** (jnp.arange(0, MLA_ROPE, 2, dtype=F32) / MLA_ROPE))
    ang = positions.astype(F32).reshape(batch * seq, 1) * inv
    ang = _pad_lanes(jnp.concatenate([ang, ang], axis=-1))

    for i in range(depth):
        j = i // 2
        if i % 2 == 0:
            x2d = _conv_layer(x2d, seq, conv_w_in[j], conv_b_in[j], conv_w_dw[j], conv_b_dw[j],
                              conv_ln_g[j], conv_ln_b[j], conv_w_out[j], conv_b_out[j],
                              ln1_g[i], ln1_b[i], alpha)
        else:
            x2d = _mla_layer(x2d, ang, batch, seq, mla_w_in[j], mla_q_norm[j], mla_kv_norm[j],
                             mla_w_uq[j], mla_w_uk[j], mla_w_uv[j], mla_w_out[j],
                             ln1_g[i], ln1_b[i], alpha)
        x2d = _peer_layer(x2d, peer_w_q[i], peer_sub_keys[i], peer_u[i], peer_v[i],
                          ln2_g[i], ln2_b[i], alpha)
    return x2d.reshape(batch, seq, d)
```

```python
import functools

import jax
import jax.numpy as jnp
from jax import lax
from jax.experimental import pallas as pl
from jax.experimental.pallas import tpu as pltpu

F32 = jnp.float32
BF16 = jnp.bfloat16
F8 = jnp.float8_e4m3fn

LANES = 128
VMEM_LIMIT = 58 * 1024 * 1024

D_MODEL = 1024
CHUNK = 64
CONV_WIDTH = 31
CONV_HALO = 32

MLA_HEADS = 8
MLA_NOPE = 128
MLA_ROPE = 64
MLA_V = 128
MLA_Q_RANK = 384
MLA_KV_RANK = 256
ROPE_THETA = 10000.0
QK_PAD = 256
ATTN_HEADS_PER_STEP = 2

PEER_HEADS = 8
PEER_N_KEYS = 128
PEER_D_QUERY = 256
PEER_TOPK = 16

LN_EPS = 1e-5
RMS_EPS = 1e-6

ROW_BLOCK = 512
EXPERT_TILE = 2048
FP8_TARGET = 224.0
FP8_MIN_BOUND = 1e-30

NT_DIMS = (((1,), (1,)), ((), ()))


def _params(n_axes, flags=None):
    return pltpu.CompilerParams(
        dimension_semantics=("arbitrary",) * n_axes, vmem_limit_bytes=VMEM_LIMIT, flags=flags)


def _layer_norm(x, g, b):
    mu = jnp.mean(x, axis=-1, keepdims=True)
    xc = x - mu
    var = jnp.mean(xc * xc, axis=-1, keepdims=True)
    return xc * lax.rsqrt(var + LN_EPS) * g + b


def _rms_norm(x, g):
    return x * lax.rsqrt(jnp.mean(x * x, axis=-1, keepdims=True) + RMS_EPS) * g


def _full(shape):
    return pl.BlockSpec(shape, lambda *_: (0,) * len(shape))


def _conv_layer_kernel(x_ref, w_in_ref, b_in_ref, w_dw_ref, b_dw_ref, lng_ref, lnb_ref,
                       w_out_ref, b_out_ref, l1g_ref, l1b_ref, o_ref, gbuf, cbuf, *,
                       blocks_per_seq, alpha):
    tm, d = x_ref.shape
    i = pl.program_id(0)

    @pl.when(i % blocks_per_seq == 0)
    def _():
        gbuf[0:CONV_HALO, :] = jnp.zeros((CONV_HALO, d), F32)

    x = x_ref[...]
    h = jnp.dot(x.astype(BF16), w_in_ref[...], preferred_element_type=F32) + b_in_ref[...]
    gbuf[CONV_HALO:CONV_HALO + tm, :] = h[:, :d] * jax.nn.sigmoid(h[:, d:])

    base = CONV_HALO - (CONV_WIDTH - 1)
    sublanes = 8
    taps_by_shift = {}
    for k in range(CONV_WIDTH):
        taps_by_shift.setdefault((base + k) % sublanes, []).append((k, (base + k) // sublanes))
    rc, lc = min(128, tm), 256
    for r0 in range(0, tm, rc):
        for l0 in range(0, d, lc):
            lanes = slice(l0, l0 + lc)
            acc = jnp.broadcast_to(b_dw_ref[:, lanes], (rc, lc))
            for shift, taps in taps_by_shift.items():
                span = sublanes * max(m for _, m in taps) + rc
                if shift:
                    n = span + sublanes
                    win = pltpu.roll(gbuf[r0:r0 + n, lanes], n - shift, axis=0)
                else:
                    win = gbuf[r0:r0 + span, lanes]
                for k, m in taps:
                    acc = acc + w_dw_ref[k:k + 1, lanes] * win[sublanes * m:sublanes * m + rc]
            cbuf[r0:r0 + rc, lanes] = acc
    gbuf[0:CONV_HALO, :] = gbuf[tm:tm + CONV_HALO, :]

    y = _layer_norm(cbuf[...], lng_ref[...], lnb_ref[...])
    y = y * jax.nn.sigmoid(y)
    mix = jnp.dot(y.astype(BF16), w_out_ref[...], preferred_element_type=F32) + b_out_ref[...]
    o_ref[...] = _layer_norm(alpha * x + mix, l1g_ref[...], l1b_ref[...])


def _conv_layer(x2d, seq, w_in, b_in, w_dw, b_dw, ln_g, ln_b, w_out, b_out, l1g, l1b, alpha):
    t, d = x2d.shape
    tm = min(ROW_BLOCK, seq)
    row = lambda v: v.reshape(1, -1)
    return pl.pallas_call(
        functools.partial(_conv_layer_kernel, blocks_per_seq=seq // tm, alpha=alpha),
        grid=(t // tm,),
        in_specs=[
            pl.BlockSpec((tm, d), lambda i: (i, 0)),
            _full((d, 2 * d)), _full((1, 2 * d)), _full((CONV_WIDTH, d)), _full((1, d)),
            _full((1, d)), _full((1, d)), _full((d, d)), _full((1, d)), _full((1, d)),
            _full((1, d)),
        ],
        out_specs=pl.BlockSpec((tm, d), lambda i: (i, 0)),
        out_shape=jax.ShapeDtypeStruct((t, d), F32),
        scratch_shapes=[pltpu.VMEM((CONV_HALO + tm, d), F32), pltpu.VMEM((tm, d), F32)],
        compiler_params=_params(1),
        name="conv_layer",
    )(x2d, w_in.astype(BF16), row(b_in), w_dw, row(b_dw), row(ln_g), row(ln_b),
      w_out.astype(BF16), row(b_out), row(l1g), row(l1b))


def _rope_kernel(ang_ref, cos_ref, sin_ref):
    ang = ang_ref[...]
    cos_ref[...] = jnp.cos(ang)
    sin_ref[...] = jnp.sin(ang)


def _rope_tables(positions):
    t = positions.size
    inv = 1.0 / (ROPE_THETA ** (jnp.arange(0, MLA_ROPE, 2, dtype=F32) / MLA_ROPE))
    ang = positions.astype(F32).reshape(t, 1) * inv
    ang = jnp.concatenate([ang, ang], axis=-1).reshape(t * MLA_ROPE // LANES, LANES)
    rows = ang.shape[0]
    tr = min(ROW_BLOCK, rows)
    spec = pl.BlockSpec((tr, LANES), lambda i: (i, 0))
    cos, sin = pl.pallas_call(
        _rope_kernel,
        grid=(rows // tr,),
        in_specs=[spec],
        out_specs=[spec, spec],
        out_shape=[jax.ShapeDtypeStruct(ang.shape, F32)] * 2,
        compiler_params=_params(1),
        name="rope_tables",
    )(ang)
    return _pad_lanes(cos.reshape(t, MLA_ROPE)), _pad_lanes(sin.reshape(t, MLA_ROPE))


def _mla_proj_kernel(x_ref, cos_ref, sin_ref, w_in_ref, qn_ref, kvn_ref, w_uq_ref, w_uk_ref, w_uv_ref,
                     q_ref, k_ref, v_ref, *, scale):
    x = x_ref[...].astype(BF16)
    c = jnp.dot(x, w_in_ref[...], preferred_element_type=F32)
    cq = _rms_norm(c[:, :MLA_Q_RANK], qn_ref[...])
    kv_end = MLA_Q_RANK + MLA_KV_RANK
    ckv = _rms_norm(c[:, MLA_Q_RANK:kv_end], kvn_ref[...])
    cos, sin = cos_ref[...], sin_ref[...]
    k_rope = c[:, kv_end:kv_end + LANES] * cos + c[:, kv_end + LANES:kv_end + 2 * LANES] * sin

    q = jnp.dot(cq.astype(BF16), w_uq_ref[...], preferred_element_type=F32)
    ckv_b = ckv.astype(BF16)
    kn = jnp.dot(ckv_b, w_uk_ref[...], preferred_element_type=F32)
    v_ref[...] = jnp.dot(ckv_b, w_uv_ref[...], preferred_element_type=F32).astype(BF16)
    k_rope_b = k_rope.astype(BF16)
    for h in range(MLA_HEADS):
        qh = q[:, h * 3 * LANES:(h + 1) * 3 * LANES]
        q_rope = qh[:, LANES:2 * LANES] * cos + qh[:, 2 * LANES:] * sin
        q_ref[:, h * QK_PAD:h * QK_PAD + LANES] = (qh[:, :LANES] * scale).astype(BF16)
        q_ref[:, h * QK_PAD + LANES:(h + 1) * QK_PAD] = (q_rope * scale).astype(BF16)
        k_ref[:, h * QK_PAD:h * QK_PAD + LANES] = kn[:, h * MLA_NOPE:(h + 1) * MLA_NOPE].astype(BF16)
        k_ref[:, h * QK_PAD + LANES:(h + 1) * QK_PAD] = k_rope_b


def _half_rotation(w):
    half = w.shape[-1] // 2
    return jnp.concatenate([-w[..., half:], w[..., :half]], axis=-1)


def _pad_lanes(w):
    return jnp.pad(w, [(0, 0)] * (w.ndim - 1) + [(0, LANES - w.shape[-1])])


def _mla_proj(x2d, ang, w_in, q_norm, kv_norm, w_uq, w_uk, w_uv):
    t, d = x2d.shape
    tm = min(ROW_BLOCK, t)
    kv_end = MLA_Q_RANK + MLA_KV_RANK
    w_rope = w_in[:, kv_end:]
    w_in_ext = jnp.concatenate(
        [w_in[:, :kv_end], _pad_lanes(w_rope), _pad_lanes(_half_rotation(w_rope))], axis=1).astype(BF16)
    wq = w_uq.reshape(MLA_Q_RANK, MLA_HEADS, MLA_NOPE + MLA_ROPE)
    wq_rope = wq[..., MLA_NOPE:]
    w_uq_ext = jnp.concatenate(
        [wq[..., :MLA_NOPE], _pad_lanes(wq_rope), _pad_lanes(_half_rotation(wq_rope))],
        axis=-1).reshape(MLA_Q_RANK, MLA_HEADS * 3 * LANES).astype(BF16)
    scale = (MLA_NOPE + MLA_ROPE) ** -0.5
    n_in = w_in_ext.shape[1]
    return pl.pallas_call(
        functools.partial(_mla_proj_kernel, scale=scale),
        grid=(t // tm,),
        in_specs=[
            pl.BlockSpec((tm, d), lambda i: (i, 0)),
            pl.BlockSpec((tm, LANES), lambda i: (i, 0)),
            pl.BlockSpec((tm, LANES), lambda i: (i, 0)),
            _full((d, n_in)), _full((1, MLA_Q_RANK)), _full((1, MLA_KV_RANK)),
            _full((MLA_Q_RANK, MLA_HEADS * 3 * LANES)),
            _full((MLA_KV_RANK, MLA_HEADS * MLA_NOPE)), _full((MLA_KV_RANK, MLA_HEADS * MLA_V)),
        ],
        out_specs=[
            pl.BlockSpec((tm, MLA_HEADS * QK_PAD), lambda i: (i, 0)),
            pl.BlockSpec((tm, MLA_HEADS * QK_PAD), lambda i: (i, 0)),
            pl.BlockSpec((tm, MLA_HEADS * MLA_V), lambda i: (i, 0)),
        ],
        out_shape=[
            jax.ShapeDtypeStruct((t, MLA_HEADS * QK_PAD), BF16),
            jax.ShapeDtypeStruct((t, MLA_HEADS * QK_PAD), BF16),
            jax.ShapeDtypeStruct((t, MLA_HEADS * MLA_V), BF16),
        ],
        compiler_params=_params(1),
        name="mla_proj",
    )(x2d, *ang, w_in_ext, q_norm.reshape(1, -1), kv_norm.reshape(1, -1), w_uq_ext,
      w_uk.astype(BF16), w_uv.astype(BF16))


def _attn_kernel(q_ref, k_ref, v_ref, o_ref):
    tq = q_ref.shape[0]
    qi = pl.program_id(2)
    heads = range(q_ref.shape[1] // QK_PAD)
    q = [q_ref[:, h * QK_PAD:(h + 1) * QK_PAD] for h in heads]

    def step(h, r0, carry, mask):
        m, l, acc = carry
        k = k_ref[pl.ds(r0, tq), h * QK_PAD:(h + 1) * QK_PAD]
        v = v_ref[pl.ds(r0, tq), h * MLA_V:(h + 1) * MLA_V]
        s = lax.dot_general(q[h], k, NT_DIMS, preferred_element_type=F32)
        if mask is not None:
            s = jnp.where(mask, s, jnp.finfo(F32).min)
        m_new = jnp.maximum(m, jnp.max(s, axis=-1, keepdims=True))
        corr = jnp.exp(m - m_new)
        p = jnp.exp(s - m_new)
        l = corr * l + jnp.sum(p, axis=-1, keepdims=True)
        acc = corr * acc + jnp.dot(p.astype(BF16), v, preferred_element_type=F32)
        return m_new, l, acc

    def body(kb, carries):
        r0 = pl.multiple_of(kb * tq, tq)
        return tuple(step(h, r0, carries[h], None) for h in heads)

    init = (jnp.full((tq, 1), -jnp.inf, F32), jnp.zeros((tq, 1), F32), jnp.zeros((tq, MLA_V), F32))
    carries = lax.fori_loop(0, qi, body, tuple(init for _ in heads))
    chunk_bits = CHUNK.bit_length() - 1
    q_chunk = lax.shift_right_logical(lax.broadcasted_iota(jnp.int32, (tq, tq), 0), chunk_bits)
    k_chunk = lax.shift_right_logical(lax.broadcasted_iota(jnp.int32, (tq, tq), 1), chunk_bits)
    r0 = pl.multiple_of(qi * tq, tq)
    for h in heads:
        _, l, acc = step(h, r0, carries[h], k_chunk <= q_chunk)
        o_ref[:, h * MLA_V:(h + 1) * MLA_V] = (acc / l).astype(BF16)


def _attention(q, k, v, batch, seq):
    t = q.shape[0]
    tq = min(ROW_BLOCK, seq)
    nq = seq // tq
    hp = ATTN_HEADS_PER_STEP
    return pl.pallas_call(
        _attn_kernel,
        grid=(batch, MLA_HEADS // hp, nq),
        in_specs=[
            pl.BlockSpec((tq, hp * QK_PAD), lambda b, h, i: (b * nq + i, h)),
            pl.BlockSpec((seq, hp * QK_PAD), lambda b, h, i: (b, h)),
            pl.BlockSpec((seq, hp * MLA_V), lambda b, h, i: (b, h)),
        ],
        out_specs=pl.BlockSpec((tq, hp * MLA_V), lambda b, h, i: (b * nq + i, h)),
        out_shape=jax.ShapeDtypeStruct((t, MLA_HEADS * MLA_V), BF16),
        compiler_params=_params(3),
        name="mla_attention",
    )(q, k, v)


def _out_proj_kernel(o_ref, x_ref, w_ref, g_ref, b_ref, y_ref, *, alpha):
    mix = jnp.dot(o_ref[...], w_ref[...], preferred_element_type=F32)
    y_ref[...] = _layer_norm(alpha * x_ref[...] + mix, g_ref[...], b_ref[...])


def _out_proj(o, x2d, w_out, g, b, alpha):
    t, d = x2d.shape
    tm = min(ROW_BLOCK, t)
    return pl.pallas_call(
        functools.partial(_out_proj_kernel, alpha=alpha),
        grid=(t // tm,),
        in_specs=[
            pl.BlockSpec((tm, o.shape[1]), lambda i: (i, 0)),
            pl.BlockSpec((tm, d), lambda i: (i, 0)),
            _full(w_out.shape), _full((1, d)), _full((1, d)),
        ],
        out_specs=pl.BlockSpec((tm, d), lambda i: (i, 0)),
        out_shape=jax.ShapeDtypeStruct((t, d), F32),
        compiler_params=_params(1),
        name="mla_out_proj",
    )(o, x2d, w_out.astype(BF16), g.reshape(1, -1), b.reshape(1, -1))


def _mla_layer(x2d, ang, batch, seq, w_in, q_norm, kv_norm, w_uq, w_uk, w_uv, w_out, l1g, l1b, alpha):
    q, k, v = _mla_proj(x2d, ang, w_in, q_norm, kv_norm, w_uq, w_uk, w_uv)
    o = _attention(q, k, v, batch, seq)
    return _out_proj(o, x2d, w_out, l1g, l1b, alpha)


def _top_k_ranked(s):
    n = s.shape[0]
    key_id = lax.broadcasted_iota(jnp.int32, s.shape, 0)
    rank = jnp.full(s.shape, float(PEER_TOPK), F32)
    vals = []
    for r in range(PEER_TOPK):
        m = jnp.max(s, axis=0, keepdims=True)
        first = jnp.min(jnp.where(s == m, key_id, n), axis=0, keepdims=True)
        hit = key_id == first
        rank = jnp.where(hit, float(r), rank)
        s = jnp.where(hit, -jnp.inf, s)
        vals.append(m)
    return jnp.concatenate(vals, axis=0), rank


def _merge_top_k(sv1, sv2):
    row_id = lax.broadcasted_iota(jnp.int32, sv1.shape, 0).astype(F32)
    count = jnp.zeros(sv1.shape, F32)
    front = sv1 + sv2[0:1]
    picked = []
    for r in range(PEER_TOPK):
        m = jnp.max(front, axis=0, keepdims=True)
        picked.append(m)
        first = jnp.min(jnp.where(front == m, row_id, float(PEER_TOPK)), axis=0, keepdims=True)
        hit = row_id == first
        count = jnp.where(hit, count + 1.0, count)
        if r + 1 < PEER_TOPK:
            new_len = jnp.sum(jnp.where(hit, count, 0.0), axis=0, keepdims=True)
            a_val = jnp.sum(jnp.where(hit, sv1, 0.0), axis=0, keepdims=True)
            b_val = jnp.sum(jnp.where(row_id == new_len, sv2, 0.0), axis=0, keepdims=True)
            nxt = jnp.where(new_len >= float(PEER_TOPK), -jnp.inf, a_val + b_val)
            front = jnp.where(hit, nxt, front)
    return count, picked


def _top_k_no_ties(s, with_rank=True):
    rank = jnp.full(s.shape, float(PEER_TOPK), F32) if with_rank else None
    vals = []
    for r in range(PEER_TOPK):
        m = jnp.max(s, axis=0, keepdims=True)
        hit = s == m
        if with_rank:
            rank = jnp.where(hit, float(r), rank)
        s = jnp.where(hit, -jnp.inf, s)
        vals.append(m)
    members = jnp.sum(jnp.where(s == -jnp.inf, 1.0, 0.0), axis=0, keepdims=True)
    return vals, rank, members


def _merge_no_ties(v1, v2):
    sub = 8
    lo2 = jnp.concatenate(v2[:sub], axis=0)
    hi2 = jnp.concatenate(v2[sub:], axis=0)
    hi1 = jnp.concatenate(v1[sub:], axis=0)
    row = lax.broadcasted_iota(jnp.int32, lo2.shape, 0)
    cand, valid = [], []
    for a in range(sub):
        lim = min(PEER_TOPK // (a + 1), sub)
        ok = row < lim
        cand.append(jnp.where(ok, v1[a] + lo2, -jnp.inf))
        valid.append(ok)
    cand.append(v1[0] + hi2)
    cand.append(hi1 + v2[0])
    valid += [None, None]
    picked = []
    for r in range(PEER_TOPK):
        m = cand[0]
        for c in cand[1:]:
            m = jnp.maximum(m, c)
        m = jnp.max(m, axis=0, keepdims=True)
        picked.append(m)
        cand = [jnp.where(c == m, -jnp.inf, c) for c in cand]
    taken = []
    for c, ok in zip(cand, valid):
        t = c == -jnp.inf
        taken.append(jnp.where(t if ok is None else t & ok, 1.0, 0.0))
    counts = [jnp.sum(t, axis=0, keepdims=True) for t in taken[:sub]]
    counts[0] = counts[0] + jnp.sum(taken[sub], axis=0, keepdims=True)
    counts += [taken[sub + 1][a:a + 1] for a in range(sub)]
    total = counts[0]
    for cnt in counts[1:]:
        total = total + cnt
    return counts, picked, total


def _fold_keys_kernel(wq_ref, keys_ref, o_ref):
    half = PEER_D_QUERY // 2
    for p in range(2):
        o_ref[p * PEER_N_KEYS:(p + 1) * PEER_N_KEYS, :] = jnp.dot(
            keys_ref[p], wq_ref[p * half:(p + 1) * half, :], preferred_element_type=F32).astype(BF16)


def _fold_keys(w_q, sub_keys):
    d = w_q.shape[0]
    rows = 2 * PEER_N_KEYS
    return pl.pallas_call(
        _fold_keys_kernel,
        grid=(PEER_HEADS,),
        in_specs=[
            pl.BlockSpec((PEER_D_QUERY, d), lambda h: (h, 0)),
            _full((2, PEER_N_KEYS, PEER_D_QUERY // 2)),
        ],
        out_specs=pl.BlockSpec((rows, d), lambda h: (h, 0)),
        out_shape=jax.ShapeDtypeStruct((PEER_HEADS * rows, d), BF16),
        compiler_params=_params(1),
        name="peer_fold_keys",
    )(w_q.T.astype(BF16), sub_keys.astype(BF16))


def _route_kernel(x_ref, ws_ref, rank2_ref, e2_ref, len1_ref, e1_ref, s_scr):
    tb = x_ref.shape[0]
    n_chunks = tb // LANES
    xb = x_ref[...].astype(BF16)
    s = lax.dot_general(ws_ref[...], xb, NT_DIMS, preferred_element_type=F32)
    for hp in range(2 * PEER_HEADS):
        for c in range(n_chunks):
            s_scr[hp, c] = s[hp * PEER_N_KEYS:(hp + 1) * PEER_N_KEYS, c * LANES:(c + 1) * LANES]

    def emit(h, c, s1, s2, in_row, rank2, top1, top2, counts, picked):
        z = jnp.ones_like(picked[0])
        for r in range(1, PEER_TOPK):
            z = z + jnp.exp(picked[r] - picked[0])
        len1 = jnp.zeros(s1.shape, F32)
        for a in range(PEER_TOPK):
            len1 = jnp.where(in_row(a), counts[a], len1)
        rank2_ref[h, c] = rank2
        e2_ref[h, c] = jnp.exp(s2 - top2)
        len1_ref[h, c] = len1
        e1_ref[h, c] = jnp.exp(s1 - top1) * (0.5 / z)

    group = min(4, n_chunks)

    def body(idx, carry):
        h = idx // (n_chunks // group)
        c0 = (idx % (n_chunks // group)) * group
        flags = []
        for dc in range(group):
            c = c0 + dc
            s1 = s_scr[2 * h, c]
            s2 = s_scr[2 * h + 1, c]
            v1, _, n1 = _top_k_no_ties(s1, with_rank=False)
            v2, rank2, n2 = _top_k_no_ties(s2)
            counts, picked, n12 = _merge_no_ties(v1, v2)
            emit(h, c, s1, s2, lambda a: s1 == v1[a], rank2, v1[0], v2[0], counts, picked)
            k = float(PEER_TOPK)
            flags.append(jnp.abs(n1 - k) + jnp.abs(n2 - k) + jnp.abs(n12 - k))

        def redo(c):
            s1 = s_scr[2 * h, c]
            s2 = s_scr[2 * h + 1, c]
            sv1, rank1 = _top_k_ranked(s1)
            sv2, rank2 = _top_k_ranked(s2)
            count, picked = _merge_top_k(sv1, sv2)
            counts = [count[a:a + 1] for a in range(PEER_TOPK)]
            emit(h, c, s1, s2, lambda a: rank1 == float(a), rank2, sv1[0:1], sv2[0:1], counts, picked)

        @pl.when(jnp.max(sum(flags)) > 0.0)
        def _():
            for dc in range(group):
                pl.when(jnp.max(flags[dc]) > 0.0)(functools.partial(redo, c0 + dc))

        return carry

    lax.fori_loop(0, PEER_HEADS * n_chunks // group, body, 0)


def _route(x2d, w_q, sub_keys):
    t, d = x2d.shape
    tb = min(ROW_BLOCK, t)
    nc = tb // LANES
    table_shape = (PEER_HEADS, t // LANES, PEER_N_KEYS, LANES)
    table_spec = pl.BlockSpec((PEER_HEADS, nc, PEER_N_KEYS, LANES), lambda i: (0, i, 0, 0))
    return pl.pallas_call(
        _route_kernel,
        grid=(t // tb,),
        in_specs=[
            pl.BlockSpec((tb, d), lambda i: (i, 0)),
            _full((PEER_HEADS * 2 * PEER_N_KEYS, d)),
        ],
        out_specs=[table_spec] * 4,
        out_shape=[jax.ShapeDtypeStruct(table_shape, F32)] * 4,
        scratch_shapes=[pltpu.VMEM((2 * PEER_HEADS, nc, PEER_N_KEYS, LANES), F32)],
        compiler_params=_params(1),
        name="peer_route",
    )(x2d, _fold_keys(w_q, sub_keys))


def _pow2_scale(bound):
    bound = jnp.maximum(bound, FP8_MIN_BOUND)
    return jnp.exp2(jnp.floor(jnp.log2(FP8_TARGET / bound)))


def _abs_max(x):
    return jnp.max(jnp.max(jnp.abs(x), axis=0, keepdims=True), axis=1, keepdims=True)


def _expert_kernel(x_ref, u_ref, v_ref, wscale_ref, rank2_ref, e2_ref, len1_ref, e1_ref, g_ref,
                   b_ref, o_ref, xt_scr, h_scr, a_scr, acc_scr, rank2_scr, e2_scr, inv_scr, *,
                   alpha, n_tiles):
    tb = x_ref.shape[0]
    te = u_ref.shape[0]
    slabs = te // PEER_N_KEYS
    n_chunks = tb // LANES
    pack = 16
    j = pl.program_id(1)

    @pl.when(j == 0)
    def _():
        x = x_ref[...]
        s_x = _pow2_scale(_abs_max(x))
        xt_scr[...] = (x * s_x).T.astype(F8)
        inv_scr[...] = jnp.broadcast_to(wscale_ref[0:1, :] / s_x, inv_scr.shape)
        acc_scr[...] = jnp.zeros(acc_scr.shape, F32)
        for h in range(PEER_HEADS):
            for c in range(n_chunks):
                rank2_scr[h, c] = rank2_ref[h, c].astype(BF16)
                e2_scr[h, c] = e2_ref[h, c].astype(BF16)

    h = jnp.dot(u_ref[...], xt_scr[...], preferred_element_type=F32)
    h_scr[...] = h
    inv_h = inv_scr[0:1, :]
    s_a = _pow2_scale(_abs_max(h) * inv_h * float(PEER_HEADS))
    gate_row = s_a * inv_h
    erf_row = inv_h * (2.0 ** -0.5)

    def slab(s):
        i1 = j * slabs + s
        r0 = s * PEER_N_KEYS
        for c in range(n_chunks):
            lanes = slice(c * LANES, (c + 1) * LANES)
            length, e1 = [], []
            for h in range(PEER_HEADS):
                row = len1_ref[h, c, pl.ds(i1, 1), :]
                length.append(jnp.broadcast_to(row, (pack, LANES)).astype(BF16))
                row = e1_ref[h, c, pl.ds(i1, 1), :] * gate_row
                e1.append(jnp.broadcast_to(row, (pack, LANES)).astype(BF16))
            outs = []
            for k in range(PEER_N_KEYS // pack):
                rows = slice(k * pack, (k + 1) * pack)
                gate = None
                for h in range(PEER_HEADS):
                    term = jnp.where(rank2_scr[h, c, rows, :] < length[h],
                                     e2_scr[h, c, rows, :], jnp.zeros((), BF16)) * e1[h]
                    gate = term if gate is None else gate + term
                raw = h_scr[r0 + k * pack:r0 + (k + 1) * pack, lanes]
                act = raw * (1.0 + lax.erf(raw * erf_row))
                outs.append(act.astype(BF16) * gate)
                if k % 2 == 1:
                    both = jnp.concatenate(outs, axis=0)
                    outs = []
                    a_scr[r0 + (k - 1) * pack:r0 + (k + 1) * pack, lanes] = both.astype(F8)

    @pl.when(j >= 0)
    def _():
        for s in range(slabs):
            slab(s)

    tn_dims = (((0,), (0,)), ((), ()))
    y = lax.dot_general(v_ref[...], a_scr[...], tn_dims, preferred_element_type=F32)
    acc_scr[...] += y * (wscale_ref[1:2, 0:1] / s_a[:, 0:1])

    @pl.when(j == n_tiles - 1)
    def _():
        y = acc_scr[...].T
        o_ref[...] = _layer_norm(alpha * x_ref[...] + y, g_ref[...], b_ref[...])


def _experts(x2d, tables, u, v, g, b, alpha):
    t, d = x2d.shape
    n_experts = u.shape[0]
    tb = min(ROW_BLOCK, t)
    te = EXPERT_TILE
    n_tiles = n_experts // te
    nc = tb // LANES
    table_spec = pl.BlockSpec((PEER_HEADS, nc, PEER_N_KEYS, LANES), lambda i, j: (0, i, 0, 0))
    s_u = _pow2_scale(jnp.max(jnp.abs(u)))
    s_v = _pow2_scale(jnp.max(jnp.abs(v)))
    wscale = jnp.broadcast_to(jnp.stack([1.0 / s_u, 1.0 / s_v])[:, None], (2, LANES))
    return pl.pallas_call(
        functools.partial(_expert_kernel, alpha=alpha, n_tiles=n_tiles),
        grid=(t // tb, n_tiles),
        in_specs=[
            pl.BlockSpec((tb, d), lambda i, j: (i, 0)),
            pl.BlockSpec((te, d), lambda i, j: (j, 0)),
            pl.BlockSpec((te, d), lambda i, j: (j, 0)),
            pl.BlockSpec((2, LANES), lambda i, j: (0, 0)),
            table_spec, table_spec, table_spec, table_spec,
            pl.BlockSpec((1, d), lambda i, j: (0, 0)),
            pl.BlockSpec((1, d), lambda i, j: (0, 0)),
        ],
        out_specs=pl.BlockSpec((tb, d), lambda i, j: (i, 0)),
        out_shape=jax.ShapeDtypeStruct((t, d), F32),
        scratch_shapes=[
            pltpu.VMEM((d, tb), F8),
            pltpu.VMEM((te, tb), F32),
            pltpu.VMEM((te, tb), F8),
            pltpu.VMEM((d, tb), F32),
            pltpu.VMEM((PEER_HEADS, nc, PEER_N_KEYS, LANES), BF16),
            pltpu.VMEM((PEER_HEADS, nc, PEER_N_KEYS, LANES), BF16),
            pltpu.VMEM((8, LANES), F32),
        ],
        compiler_params=_params(2),
        name="peer_experts",
    )(x2d, (u * s_u).astype(F8), (v * s_v).astype(F8), wscale, *tables,
      g.reshape(1, -1), b.reshape(1, -1))


def _peer_layer(x2d, w_q, sub_keys, u, v, g, b, alpha):
    tables = _route(x2d, w_q, sub_keys)
    return _experts(x2d, tables, u, v, g, b, alpha)


def kernel(x, positions, conv_w_in, conv_b_in, conv_w_dw, conv_b_dw, conv_ln_g, conv_ln_b,
           conv_w_out, conv_b_out, mla_w_in, mla_q_norm, mla_kv_norm, mla_w_uq, mla_w_uk,
           mla_w_uv, mla_w_out, peer_w_q, peer_sub_keys, peer_u, peer_v,
           ln1_g, ln1_b, ln2_g, ln2_b):
    batch, seq, d = x.shape
    depth = ln1_g.shape[0]
    alpha = (2 * depth) ** 0.25
    x2d = x.reshape(batch * seq, d)

    ang = _rope_tables(positions)

    for i in range(depth):
        j = i // 2
        if i % 2 == 0:
            x2d = _conv_layer(x2d, seq, conv_w_in[j], conv_b_in[j], conv_w_dw[j], conv_b_dw[j],
                              conv_ln_g[j], conv_ln_b[j], conv_w_out[j], conv_b_out[j],
                              ln1_g[i], ln1_b[i], alpha)
        else:
            x2d = _mla_layer(x2d, ang, batch, seq, mla_w_in[j], mla_q_norm[j], mla_kv_norm[j],
                             mla_w_uq[j], mla_w_uk[j], mla_w_uv[j], mla_w_out[j],
                             ln1_g[i], ln1_b[i], alpha)
        x2d = _peer_layer(x2d, peer_w_q[i], peer_sub_keys[i], peer_u[i], peer_v[i],
                          ln2_g[i], ln2_b[i], alpha)
    return x2d.reshape(batch, seq, d)
```
